```python
import math
import jax
import jax.numpy as jnp
from jax import lax
import numpy as np

D_MODEL = 1024
BATCH = 16
SEQ = 2048
DEPTH = 1

GRID_W = 64
CTX_LEN = 256
DA_HEADS = 8
DA_HEAD_DIM = 64
DA_V_DIM = 2 * DA_HEAD_DIM
DA_WIDTH = DA_HEADS * 2 * DA_HEAD_DIM
LRU_WIDTH = D_MODEL
LRU_BLOCKS = 16
LRU_BLOCK_DIM = LRU_WIDTH // LRU_BLOCKS
CONV_WIDTH = 4
CONV_PAD_LEFT = (CONV_WIDTH - 1) // 2
CONV_PAD_RIGHT = CONV_WIDTH - 1 - CONV_PAD_LEFT
LRU_C = 8.0
ROPE_THETA = 10000.0
Q_BLOCK = 128
NORM_EPS = 1e-6
N_BRANCHES = 2
IN_COLS = 4 * DA_WIDTH + 2 * LRU_WIDTH + N_BRANCHES * D_MODEL

kernel_name = 'hybrid_diffattn_rglru_prefix_dit_block'


def _rmsnorm(x, g):
    xf = x.astype(jnp.float32)
    y = xf * lax.rsqrt(jnp.mean(xf * xf, axis=-1, keepdims=True) + NORM_EPS)
    return (y * g.astype(jnp.float32)).astype(x.dtype)


def _lambda_init(layer_idx):
    return 0.8 - 0.6 * math.exp(-0.3 * layer_idx)


def _split_in(p):
    widths = (DA_WIDTH, DA_WIDTH, DA_WIDTH, DA_WIDTH, LRU_WIDTH, LRU_WIDTH, N_BRANCHES * D_MODEL)
    offsets = np.cumsum(widths)[:-1].tolist()
    return jnp.split(p, offsets, axis=-1)


def _axial_rope_tables(n_tokens):
    rows = n_tokens // GRID_W
    row = jnp.repeat(jnp.arange(rows, dtype=jnp.float32), GRID_W)
    col = jnp.tile(jnp.arange(GRID_W, dtype=jnp.float32), rows)
    axis_dim = DA_HEAD_DIM // 2
    inv_freq = ROPE_THETA ** (-jnp.arange(0, axis_dim, 2, dtype=jnp.float32) / axis_dim)
    ang_r = row[:, None] * inv_freq[None, :]
    ang_c = col[:, None] * inv_freq[None, :]
    return (jnp.cos(ang_r), jnp.sin(ang_r), jnp.cos(ang_c), jnp.sin(ang_c))


def _rotate(x, cos, sin):
    x1, x2 = jnp.split(x, 2, axis=-1)
    cs = cos[:, None, None, :]
    sn = sin[:, None, None, :]
    return jnp.concatenate([x1 * cs - x2 * sn, x1 * sn + x2 * cs], axis=-1)


def _apply_axial_rope(x, tables):
    cr, sr, cc, sc = tables
    half = DA_HEAD_DIM // 2
    xf = x.astype(jnp.float32)
    out = jnp.concatenate([_rotate(xf[..., :half], cr, sr), _rotate(xf[..., half:], cc, sc)], axis=-1)
    return out.astype(x.dtype)


def _diff_attend(q, k, v, lam):
    s = jnp.einsum('bqhjd,bkhjd->bhjqk', q, k, preferred_element_type=jnp.float32) * (DA_HEAD_DIM ** -0.5)
    p = jax.nn.softmax(s, axis=-1)
    w = p[:, :, 0] - lam * p[:, :, 1]
    return jnp.einsum('bhqk,bkhe->bqhe', w.astype(v.dtype), v)


def _latent_diff_attention(q_lat, k_all, v_all, lam):
    b, n = q_lat.shape[:2]
    n_blk = n // Q_BLOCK
    qb = q_lat.reshape(b, n_blk, Q_BLOCK, DA_HEADS, 2, DA_HEAD_DIM).swapaxes(0, 1)
    out = lax.map(lambda qq: _diff_attend(qq, k_all, v_all, lam), qb)
    return out.swapaxes(0, 1).reshape(b, n, DA_HEADS, DA_V_DIM)


def _diff_head_out(o, g_subln, lam_init):
    o = _rmsnorm(o, g_subln) * (1.0 - lam_init)
    return o.reshape(o.shape[0], o.shape[1], DA_WIDTH)


def _centred_dwconv(x, w, b):
    n = x.shape[1]
    xp = jnp.pad(x, ((0, 0), (CONV_PAD_LEFT, CONV_PAD_RIGHT), (0, 0)))
    y = b
    for j in range(CONV_WIDTH):
        y = y + xp[:, j:j + n] * w[j]
    return y


def _rglru_coeffs(xc, w_a, b_a, w_x, b_x, lam):
    b, n, _ = xc.shape
    xb = xc.reshape(b, n, LRU_BLOCKS, LRU_BLOCK_DIM)
    r = jax.nn.sigmoid((jnp.einsum('blni,nij->blnj', xb, w_a).reshape(b, n, LRU_WIDTH) + b_a).astype(jnp.float32))
    i = jax.nn.sigmoid((jnp.einsum('blni,nij->blnj', xb, w_x).reshape(b, n, LRU_WIDTH) + b_x).astype(jnp.float32))
    log_a = -LRU_C * r * jax.nn.softplus(-lam.astype(jnp.float32))
    a = jnp.exp(log_a)
    mult = jnp.sqrt(-jnp.expm1(2.0 * log_a))
    return a, mult * i * xc.astype(jnp.float32)


def _linear_scan(a, u, h0):
    def combine(e1, e2):
        return (e1[0] * e2[0], e2[0] * e1[1] + e2[1])
    a_cum, u_cum = lax.associative_scan(combine, (a, u), axis=1)
    return a_cum * h0[:, None, :] + u_cum


def _rglru_direction(xc_ctx, xc_lat, w_a, b_a, w_x, b_x, lam, reverse):
    a_c, u_c = _rglru_coeffs(xc_ctx, w_a, b_a, w_x, b_x, lam)
    a_l, u_l = _rglru_coeffs(xc_lat, w_a, b_a, w_x, b_x, lam)
    if reverse:
        a_c, u_c, a_l, u_l = (jnp.flip(a_c, 1), jnp.flip(u_c, 1), jnp.flip(a_l, 1), jnp.flip(u_l, 1))
    h_c = _linear_scan(a_c, u_c, jnp.zeros_like(u_c[:, 0]))
    h_l = _linear_scan(a_l, u_l, h_c[:, -1])
    if reverse:
        h_c, h_l = (jnp.flip(h_c, 1), jnp.flip(h_l, 1))
    return h_c, h_l


def _merge_branches(attn, lru, gate_attn, gate_lru, gate_merge, w_attn_out, w_lru_out, w_out):
    y_attn = (attn * jax.nn.silu(gate_attn)) @ w_attn_out
    y_lru = (lru * jax.nn.silu(gate_lru)) @ w_lru_out
    m_attn, m_lru = jnp.split(jax.nn.sigmoid(gate_merge), N_BRANCHES, axis=-1)
    return (m_attn * y_attn + m_lru * y_lru) @ w_out


def setup_inputs(seed: int = 0) -> dict:
    key = jax.random.key(seed)
    ks = jax.random.split(key, 24)
    f32 = jnp.float32

    def nrm(k, shape, scale):
        return jax.random.normal(k, shape, f32) * scale

    u = jax.random.uniform(ks[21], (DEPTH, 2, LRU_WIDTH), f32, 0.9, 0.999)
    a0 = u ** (1.0 / LRU_C)
    return {
        'x': nrm(ks[0], (BATCH, SEQ, D_MODEL), 1.0),
        'c': nrm(ks[1], (BATCH, D_MODEL), 1.0),
        'ctx': nrm(ks[2], (BATCH, CTX_LEN, D_MODEL), 1.0),
        'c_ctx': nrm(ks[3], (D_MODEL,), 1.0),
        'w_mod': nrm(ks[4], (DEPTH, D_MODEL, 3 * D_MODEL), D_MODEL ** -0.5),
        'b_mod': nrm(ks[5], (DEPTH, 3 * D_MODEL), 0.02),
        'g_pre': 1.0 + nrm(ks[6], (DEPTH, D_MODEL), 0.05),
        'g_post': 1.0 + nrm(ks[7], (DEPTH, D_MODEL), 0.05),
        'w_in': nrm(ks[8], (DEPTH, D_MODEL, IN_COLS), D_MODEL ** -0.5),
        'lambda_q1': nrm(ks[9], (DEPTH, DA_HEAD_DIM), 0.1),
        'lambda_k1': nrm(ks[10], (DEPTH, DA_HEAD_DIM), 0.1),
        'lambda_q2': nrm(ks[11], (DEPTH, DA_HEAD_DIM), 0.1),
        'lambda_k2': nrm(ks[12], (DEPTH, DA_HEAD_DIM), 0.1),
        'g_subln': 1.0 + nrm(ks[13], (DEPTH, DA_V_DIM), 0.05),
        'w_attn_out': nrm(ks[14], (DEPTH, DA_WIDTH, D_MODEL), DA_WIDTH ** -0.5),
        'conv_w': nrm(ks[15], (DEPTH, CONV_WIDTH, LRU_WIDTH), CONV_WIDTH ** -0.5),
        'conv_b': nrm(ks[16], (DEPTH, LRU_WIDTH), 0.02),
        'w_rg_a': nrm(ks[17], (DEPTH, 2, LRU_BLOCKS, LRU_BLOCK_DIM, LRU_BLOCK_DIM), LRU_BLOCK_DIM ** -0.5),
        'b_rg_a': nrm(ks[18], (DEPTH, 2, LRU_WIDTH), 0.02),
        'w_rg_x': nrm(ks[19], (DEPTH, 2, LRU_BLOCKS, LRU_BLOCK_DIM, LRU_BLOCK_DIM), LRU_BLOCK_DIM ** -0.5),
        'b_rg_x': nrm(ks[20], (DEPTH, 2, LRU_WIDTH), 0.02),
        'lru_lambda': jnp.log(a0) - jnp.log1p(-a0),
        'w_lru_out': nrm(ks[22], (DEPTH, LRU_WIDTH, D_MODEL), LRU_WIDTH ** -0.5),
        'w_out': nrm(ks[23], (DEPTH, D_MODEL, D_MODEL), D_MODEL ** -0.5),
    }


def reference(x, c, ctx, c_ctx, w_mod, b_mod, g_pre, g_post, w_in, lambda_q1, lambda_k1, lambda_q2, lambda_k2,
              g_subln, w_attn_out, conv_w, conv_b, w_rg_a, b_rg_a, w_rg_x, b_rg_x, lru_lambda, w_lru_out, w_out):
    b, n = x.shape[:2]
    nc = ctx.shape[1]
    rope = _axial_rope_tables(n)
    silu_c = jax.nn.silu(c)
    silu_cc = jax.nn.silu(c_ctx)
    for l in range(DEPTH):
        last = l == DEPTH - 1
        shift_l, scale_l, gate_l = jnp.split((silu_c @ w_mod[l] + b_mod[l])[:, None, :], 3, axis=-1)
        shift_c, scale_c, gate_c = jnp.split(silu_cc @ w_mod[l] + b_mod[l], 3, axis=-1)
        h = _rmsnorm(x, g_pre[l]) * (1.0 + scale_l) + shift_l
        hc = _rmsnorm(ctx, g_pre[l]) * (1.0 + scale_c) + shift_c
        q, k, v, ga, xr, gr, gm = _split_in(h @ w_in[l])
        qc, kc, vc, gac, xrc, grc, gmc = _split_in(hc @ w_in[l])
        q = _apply_axial_rope(q.reshape(b, n, DA_HEADS, 2, DA_HEAD_DIM), rope)
        k = _apply_axial_rope(k.reshape(b, n, DA_HEADS, 2, DA_HEAD_DIM), rope)
        v = v.reshape(b, n, DA_HEADS, DA_V_DIM)
        kc = kc.reshape(b, nc, DA_HEADS, 2, DA_HEAD_DIM)
        vc = vc.reshape(b, nc, DA_HEADS, DA_V_DIM)
        lam_init = _lambda_init(l)
        lam = (jnp.exp(jnp.sum(lambda_q1[l].astype(jnp.float32) * lambda_k1[l].astype(jnp.float32)))
               - jnp.exp(jnp.sum(lambda_q2[l].astype(jnp.float32) * lambda_k2[l].astype(jnp.float32))) + lam_init)
        k_all = jnp.concatenate([kc, k], axis=1)
        v_all = jnp.concatenate([vc, v], axis=1)
        attn_lat = _diff_head_out(_latent_diff_attention(q, k_all, v_all, lam), g_subln[l], lam_init)
        xr_l = _centred_dwconv(xr, conv_w[l], conv_b[l])
        xr_c = _centred_dwconv(xrc, conv_w[l], conv_b[l])
        hf_c, hf_l = _rglru_direction(xr_c, xr_l, w_rg_a[l, 0], b_rg_a[l, 0], w_rg_x[l, 0], b_rg_x[l, 0], lru_lambda[l, 0], False)
        hb_c, hb_l = _rglru_direction(xr_c, xr_l, w_rg_a[l, 1], b_rg_a[l, 1], w_rg_x[l, 1], b_rg_x[l, 1], lru_lambda[l, 1], True)
        lru_lat = (hf_l + hb_l).astype(x.dtype)
        y = _merge_branches(attn_lat, lru_lat, ga, gr, gm, w_attn_out[l], w_lru_out[l], w_out[l])
        x_new = x + gate_l * _rmsnorm(y, g_post[l])
        if not last:
            qc = qc.reshape(b, nc, DA_HEADS, 2, DA_HEAD_DIM)
            attn_ctx = _diff_head_out(_diff_attend(qc, kc, vc, lam), g_subln[l], lam_init)
            lru_ctx = (hf_c + hb_c).astype(ctx.dtype)
            yc = _merge_branches(attn_ctx, lru_ctx, gac, grc, gmc, w_attn_out[l], w_lru_out[l], w_out[l])
            ctx = ctx + gate_c * _rmsnorm(yc, g_post[l])
        x = x_new
    return x
```

```python
import functools
import math

import jax
import jax.numpy as jnp
from jax import lax
from jax.experimental import pallas as pl
from jax.experimental.pallas import tpu as pltpu

F32 = jnp.float32
BF16 = jnp.bfloat16

LANES = 128
SUBLANES = 8
BF16_ROWS = 16
VMEM_BYTES = 64 * 1024 * 1024

GRID_W = 64
HEADS = 8
HEAD_DIM = 64
V_DIM = 2 * HEAD_DIM
LRU_BLOCKS = 16
CONV_WIDTH = 4
LRU_C = 8.0
ROPE_THETA = 10000.0
NORM_EPS = 1e-6
LAMBDA_INIT = 0.8 - 0.6 * math.exp(-0.3 * 0)

MOD_ROWS = 24
PROJ_TM = 256
ATTN_TQ = 256
LRU_TC = 128
LRU_WC = 256
LRU_PITCH = LRU_TC + SUBLANES


def _sigmoid(x):
    return 0.5 * jnp.tanh(0.5 * x) + 0.5


def _vmem_limit(nbytes):
    return int(min(nbytes, VMEM_BYTES - 8 * 1024 * 1024))


def _mod_kernel(c_ref, w_ref, b_ref, lq1_ref, lk1_ref, lq2_ref, lk2_ref, mod_ref, lam_ref):
    c = c_ref[...]
    s = c * _sigmoid(c)
    mod_ref[...] = jnp.dot(s, w_ref[...], precision=lax.Precision.HIGHEST,
                           preferred_element_type=F32) + b_ref[...]
    s1 = jnp.sum(lq1_ref[...] * lk1_ref[...], axis=-1, keepdims=True)
    s2 = jnp.sum(lq2_ref[...] * lk2_ref[...], axis=-1, keepdims=True)
    lam = jnp.exp(s1) - jnp.exp(s2) + LAMBDA_INIT
    lam_ref[...] = jnp.broadcast_to(lam, lam_ref.shape)


def _modulation(c_all, w_mod, b_mod, lq1, lk1, lq2, lk2):
    d = c_all.shape[1]
    n_out = w_mod.shape[1]
    tn = 512
    vec = pl.BlockSpec((1, HEAD_DIM), lambda j: (0, 0))
    return pl.pallas_call(
        _mod_kernel,
        grid=(n_out // tn,),
        in_specs=[
            pl.BlockSpec((MOD_ROWS, d), lambda j: (0, 0)),
            pl.BlockSpec((d, tn), lambda j: (0, j)),
            pl.BlockSpec((1, tn), lambda j: (0, j)),
            vec, vec, vec, vec,
        ],
        out_specs=[
            pl.BlockSpec((MOD_ROWS, tn), lambda j: (0, j)),
            pl.BlockSpec((1, LANES), lambda j: (0, 0)),
        ],
        out_shape=[
            jax.ShapeDtypeStruct((MOD_ROWS, n_out), F32),
            jax.ShapeDtypeStruct((1, LANES), F32),
        ],
        name="mod",
    )(c_all, w_mod, b_mod, lq1, lk1, lq2, lk2)


def _rope_slab(xs, cos, sin, low_half):
    up = pltpu.roll(xs, LANES - 16, 1)
    dn = pltpu.roll(xs, 16, 1)
    return xs * cos + jnp.where(low_half, up, dn) * sin


def _proj_kernel(x_ref, mod_ref, g_ref, w_ref, qcos_ref, qsin_ref, kcos_ref, ksin_ref, *out_refs,
                 d_model, groups):
    x = x_ref[...]
    ms = jnp.mean(x * x, axis=-1, keepdims=True)
    y = x * lax.rsqrt(ms + NORM_EPS) * g_ref[...]
    shift = mod_ref[:, 0:d_model]
    scale = mod_ref[:, d_model:2 * d_model]
    h = (y * (1.0 + scale) + shift).astype(BF16)
    lane = lax.broadcasted_iota(jnp.int32, (x.shape[0], LANES), 1)
    low_half = (lane & 16) == 0
    off = 0
    for (width, rope), o_ref in zip(groups, out_refs):
        r = jnp.dot(h, w_ref[:, off:off + width], preferred_element_type=F32)
        if rope is None:
            o_ref[...] = r.astype(o_ref.dtype)
        else:
            cos_ref, sin_ref = (qcos_ref, qsin_ref) if rope == "q" else (kcos_ref, ksin_ref)
            cos = cos_ref[...]
            sin = sin_ref[...]
            for s in range(width // LANES):
                sl = slice(s * LANES, (s + 1) * LANES)
                o_ref[:, sl] = _rope_slab(r[:, sl], cos, sin, low_half).astype(o_ref.dtype)
        off += width


def _project(x2, mod3, g_pre, w, tables, groups, rows_per_batch, mod_row_of_tile):
    n, d = x2.shape
    tm = PROJ_TM
    tiles_per_batch = rows_per_batch // tm
    n_cols = w.shape[1]
    tab = pl.BlockSpec((tm, LANES), lambda i: (i % tiles_per_batch, 0))
    out_cols = sum(wd for wd, _ in groups)
    est = (d * n_cols * 2 + 2 * tm * d * 4 + 2 * tm * out_cols * 2 + 6 * tm * d * 4
           + 8 * tm * LANES * 4 + (4 << 20))
    return pl.pallas_call(
        functools.partial(_proj_kernel, d_model=d, groups=groups),
        grid=(n // tm,),
        in_specs=[
            pl.BlockSpec((tm, d), lambda i: (i, 0)),
            pl.BlockSpec((None, 1, 3 * d), lambda i: (mod_row_of_tile(i, tiles_per_batch), 0, 0)),
            pl.BlockSpec((1, d), lambda i: (0, 0)),
            pl.BlockSpec((d, n_cols), lambda i: (0, 0), pipeline_mode=pl.Buffered(1)),
            tab, tab, tab, tab,
        ],
        out_specs=[pl.BlockSpec((tm, wd), lambda i: (i, 0)) for wd, _ in groups],
        out_shape=[jax.ShapeDtypeStruct((n, wd), BF16) for wd, _ in groups],
        compiler_params=pltpu.CompilerParams(
            dimension_semantics=("arbitrary",), vmem_limit_bytes=_vmem_limit(est)),
        name="proj",
    )(x2, mod3, g_pre, w, *tables)


def _attn_kernel(q_ref, kc_ref, k_ref, vc_ref, v_ref, lam_ref, g_ref, o_ref, kall_ref, vext_ref,
                 *, n_ctx):
    @pl.when(pl.program_id(2) == 0)
    def _():
        kall_ref[0:n_ctx, :] = kc_ref[...]
        kall_ref[n_ctx:, :] = k_ref[...]
        vext_ref[0:n_ctx, 0:V_DIM] = vc_ref[...]
        vext_ref[n_ctx:, 0:V_DIM] = v_ref[...]
        vext_ref[:, V_DIM:] = jnp.ones((vext_ref.shape[0], V_DIM), BF16)

    q = q_ref[...]
    lane = lax.broadcasted_iota(jnp.int32, q.shape, 1)
    kall = kall_ref[...]
    vext = vext_ref[...]
    outs = []
    for j in range(2):
        in_half = (lane < HEAD_DIM) if j == 0 else (lane >= HEAD_DIM)
        qj = jnp.where(in_half, q, jnp.zeros_like(q))
        s = lax.dot_general(qj, kall, (((1,), (1,)), ((), ())), preferred_element_type=F32)
        m = jnp.max(s, axis=-1, keepdims=True)
        e = jnp.exp2(s - m).astype(BF16)
        acc = jnp.dot(e, vext, preferred_element_type=F32)
        outs.append(acc[:, 0:V_DIM] / acc[:, V_DIM:])
    o = outs[0] - lam_ref[...] * outs[1]
    ms = jnp.mean(o * o, axis=-1, keepdims=True)
    o = o * lax.rsqrt(ms + NORM_EPS) * g_ref[...] * (1.0 - LAMBDA_INIT)
    o_ref[...] = o.astype(o_ref.dtype)


def _attention(q, kc, k, vc, v, lam_row, g_subln, batch, n_lat, n_ctx):
    tq = ATTN_TQ
    nq = n_lat // tq
    n_keys = n_ctx + n_lat
    est = (2 * (n_keys * V_DIM * 2 * 2) + n_keys * V_DIM * 2 + n_keys * 2 * V_DIM * 2
           + 4 * tq * V_DIM * 2 + 2 * (tq * n_keys * 6) + (8 << 20))
    return pl.pallas_call(
        functools.partial(_attn_kernel, n_ctx=n_ctx),
        grid=(batch, HEADS, nq),
        in_specs=[
            pl.BlockSpec((tq, V_DIM), lambda b, h, i: (b * nq + i, h)),
            pl.BlockSpec((n_ctx, V_DIM), lambda b, h, i: (b, h)),
            pl.BlockSpec((n_lat, V_DIM), lambda b, h, i: (b, h)),
            pl.BlockSpec((n_ctx, V_DIM), lambda b, h, i: (b, h)),
            pl.BlockSpec((n_lat, V_DIM), lambda b, h, i: (b, h)),
            pl.BlockSpec((1, LANES), lambda b, h, i: (0, 0)),
            pl.BlockSpec((1, V_DIM), lambda b, h, i: (0, 0)),
        ],
        out_specs=pl.BlockSpec((tq, V_DIM), lambda b, h, i: (b * nq + i, h)),
        out_shape=jax.ShapeDtypeStruct((batch * n_lat, HEADS * V_DIM), BF16),
        scratch_shapes=[
            pltpu.VMEM((n_keys, V_DIM), BF16),
            pltpu.VMEM((n_keys, 2 * V_DIM), BF16),
        ],
        compiler_params=pltpu.CompilerParams(
            dimension_semantics=("arbitrary", "arbitrary", "arbitrary"),
            vmem_limit_bytes=_vmem_limit(est)),
        name="attn",
    )(q, kc, k, vc, v, lam_row, g_subln)


def _lru_kernel(xf_ref, xfp_ref, xfn_ref, xb_ref, xbp_ref, xbn_ref, cw_ref, cb_ref, wg_ref, bg_ref,
                lam_ref, h0_ref, hf_ref, hb_ref, hfin_ref, xe_ref, a_ref, u_ref, h_ref, st_ref,
                *, batch, tc, wc):
    i = pl.program_id(1)
    n = pl.num_programs(1)
    n_slab = wc // LANES
    n_grp = batch // SUBLANES
    pitch = LRU_PITCH
    bgrp = 4

    @pl.when(i == 0)
    def _():
        st_ref[...] = h0_ref[...]

    def coeffs(d, cur_ref, prev_ref, next_ref, chunk):
        xe_ref[:, SUBLANES:SUBLANES + tc, :] = cur_ref[...].astype(F32)
        prev = prev_ref[...].astype(F32)[:, BF16_ROWS - SUBLANES:, :]
        xe_ref[:, 0:SUBLANES, :] = jnp.where(chunk == 0, 0.0, prev)
        nxt = next_ref[...].astype(F32)[:, 0:SUBLANES, :]
        xe_ref[:, SUBLANES + tc:, :] = jnp.where(chunk == n - 1, 0.0, nxt)
        cl = -LRU_C * jax.nn.softplus(-lam_ref[d])
        for b0 in range(0, batch, bgrp):
            xc = cb_ref[...]
            for j in range(CONV_WIDTH):
                lo = SUBLANES - 1 + j
                xc = xc + xe_ref[b0:b0 + bgrp, lo:lo + tc, :] * cw_ref[j:j + 1, :]
            xc = xc.reshape(bgrp * tc, wc)
            g = jnp.dot(xc.astype(BF16), wg_ref[d], preferred_element_type=F32) + bg_ref[d]
            r = _sigmoid(g[:, 0:wc])
            ig = _sigmoid(g[:, wc:])
            a = jnp.exp(cl * r)
            u = jnp.sqrt(1.0 - a * a) * ig * xc
            for bb in range(bgrp):
                for l in range(n_slab):
                    rows = slice(bb * tc, (bb + 1) * tc)
                    cols = slice(l * LANES, (l + 1) * LANES)
                    dst = pl.ds((b0 + bb) * pitch, tc)
                    a_ref[d, l, dst, :] = a[rows, cols]
                    u_ref[d, l, dst, :] = u[rows, cols]

    coeffs(0, xf_ref, xfp_ref, xfn_ref, i)
    coeffs(1, xb_ref, xbp_ref, xbn_ref, n - 1 - i)

    def load_state(d):
        return [[st_ref[d, g * SUBLANES:(g + 1) * SUBLANES, l * LANES:(l + 1) * LANES]
                 for g in range(n_grp)] for l in range(n_slab)]

    def step(t, carry):
        hs = [carry[0], carry[1]]
        times = (t, tc - 1 - t)
        new = []
        for d in range(2):
            hd = []
            for l in range(n_slab):
                hl = []
                for g in range(n_grp):
                    rows = pl.ds(g * SUBLANES * pitch + times[d], SUBLANES, stride=pitch)
                    hv = a_ref[d, l, rows, :] * hs[d][l][g] + u_ref[d, l, rows, :]
                    h_ref[d, l, rows, :] = hv
                    hl.append(hv)
                hd.append(hl)
            new.append(hd)
        return new[0], new[1]

    fin = lax.fori_loop(0, tc, step, (load_state(0), load_state(1)), unroll=8)
    for d in range(2):
        for l in range(n_slab):
            for g in range(n_grp):
                st_ref[d, g * SUBLANES:(g + 1) * SUBLANES, l * LANES:(l + 1) * LANES] = fin[d][l][g]

    for d, o_ref in ((0, hf_ref), (1, hb_ref)):
        for b in range(batch):
            for l in range(n_slab):
                o_ref[b, :, l * LANES:(l + 1) * LANES] = (
                    h_ref[d, l, b * pitch:b * pitch + tc, :].astype(o_ref.dtype))

    @pl.when(i == n - 1)
    def _():
        hfin_ref[...] = st_ref[...]


def _rglru(xr3, conv_w, conv_b, wg, bg, lam, h0):
    batch, length, width = xr3.shape
    tc, wc = LRU_TC, LRU_WC
    n = length // tc
    hb_per = tc // BF16_ROWS
    n_halo = length // BF16_ROWS
    n_slab = wc // LANES

    def cur(fwd):
        return pl.BlockSpec((batch, tc, wc), lambda s, i: (0, i if fwd else n - 1 - i, s))

    def prev(fwd):
        return pl.BlockSpec((batch, BF16_ROWS, wc), lambda s, i: (
            0, jnp.maximum((i if fwd else n - 1 - i) * hb_per - 1, 0), s))

    def nxt(fwd):
        return pl.BlockSpec((batch, BF16_ROWS, wc), lambda s, i: (
            0, jnp.minimum(((i if fwd else n - 1 - i) + 1) * hb_per, n_halo - 1), s))

    scan_scratch = pltpu.VMEM((2, n_slab, batch * LRU_PITCH, LANES), F32)
    est = (3 * 2 * n_slab * batch * LRU_PITCH * LANES * 4 + batch * (tc + 2 * SUBLANES) * wc * 4
           + 8 * batch * tc * wc * 2 + 16 * 4 * tc * wc * 4 + (8 << 20))
    return pl.pallas_call(
        functools.partial(_lru_kernel, batch=batch, tc=tc, wc=wc),
        grid=(width // wc, n),
        in_specs=[
            cur(True), prev(True), nxt(True), cur(False), prev(False), nxt(False),
            pl.BlockSpec((CONV_WIDTH, wc), lambda s, i: (0, s)),
            pl.BlockSpec((1, wc), lambda s, i: (0, s)),
            pl.BlockSpec((2, None, wc, 2 * wc), lambda s, i: (0, s, 0, 0)),
            pl.BlockSpec((2, None, 1, 2 * wc), lambda s, i: (0, s, 0, 0)),
            pl.BlockSpec((2, 1, wc), lambda s, i: (0, 0, s)),
            pl.BlockSpec((2, batch, wc), lambda s, i: (0, 0, s)),
        ],
        out_specs=[
            pl.BlockSpec((batch, tc, wc), lambda s, i: (0, i, s)),
            pl.BlockSpec((batch, tc, wc), lambda s, i: (0, n - 1 - i, s)),
            pl.BlockSpec((2, batch, wc), lambda s, i: (0, 0, s)),
        ],
        out_shape=[
            jax.ShapeDtypeStruct((batch, length, width), BF16),
            jax.ShapeDtypeStruct((batch, length, width), BF16),
            jax.ShapeDtypeStruct((2, batch, width), F32),
        ],
        scratch_shapes=[
            pltpu.VMEM((batch, tc + 2 * SUBLANES, wc), F32),
            scan_scratch, scan_scratch, scan_scratch,
            pltpu.VMEM((2, batch, wc), F32),
        ],
        compiler_params=pltpu.CompilerParams(
            dimension_semantics=("arbitrary", "arbitrary"), vmem_limit_bytes=_vmem_limit(est)),
        name="lru",
    )(xr3, xr3, xr3, xr3, xr3, xr3, conv_w, conv_b, wg, bg, lam, h0)


def _merge_kernel(attn_ref, ga_ref, hf_ref, hb_ref, gr_ref, gm_ref, x_ref, mod_ref, gp_ref,
                  wa_ref, wl_ref, wo_ref, o_ref, *, d_model):
    def silu(ref):
        g = ref[...].astype(F32)
        return g * _sigmoid(g)

    a_in = (attn_ref[...].astype(F32) * silu(ga_ref)).astype(BF16)
    y_attn = jnp.dot(a_in, wa_ref[...], preferred_element_type=F32)
    lru = hf_ref[...].astype(F32) + hb_ref[...].astype(F32)
    l_in = (lru * silu(gr_ref)).astype(BF16)
    y_lru = jnp.dot(l_in, wl_ref[...], preferred_element_type=F32)
    m_attn = _sigmoid(gm_ref[:, 0:d_model].astype(F32))
    m_lru = _sigmoid(gm_ref[:, d_model:].astype(F32))
    z = (m_attn * y_attn + m_lru * y_lru).astype(BF16)
    y = jnp.dot(z, wo_ref[...], preferred_element_type=F32)
    ms = jnp.mean(y * y, axis=-1, keepdims=True)
    yn = y * lax.rsqrt(ms + NORM_EPS) * gp_ref[...]
    gate = mod_ref[:, 2 * d_model:]
    o_ref[...] = x_ref[...] + gate * yn


def _merge(attn, ga, hf, hb, gr, gm, x2, mod3, g_post, wa, wl, wo, rows_per_batch):
    n, d = x2.shape
    tm = PROJ_TM
    tiles_per_batch = rows_per_batch // tm
    row = pl.BlockSpec((tm, d), lambda i: (i, 0))
    wspec = pl.BlockSpec((d, d), lambda i: (0, 0), pipeline_mode=pl.Buffered(1))
    est = 3 * d * d * 2 + 2 * (5 * tm * d * 2 + tm * 2 * d * 2 + 2 * tm * d * 4) + 10 * tm * d * 4 + (4 << 20)
    return pl.pallas_call(
        functools.partial(_merge_kernel, d_model=d),
        grid=(n // tm,),
        in_specs=[
            row, row, row, row, row,
            pl.BlockSpec((tm, 2 * d), lambda i: (i, 0)),
            row,
            pl.BlockSpec((None, 1, 3 * d), lambda i: (i // tiles_per_batch, 0, 0)),
            pl.BlockSpec((1, d), lambda i: (0, 0)),
            wspec, wspec, wspec,
        ],
        out_specs=row,
        out_shape=jax.ShapeDtypeStruct((n, d), F32),
        compiler_params=pltpu.CompilerParams(
            dimension_semantics=("arbitrary",), vmem_limit_bytes=_vmem_limit(est)),
        name="merge",
    )(attn, ga, hf, hb, gr, gm, x2, mod3, g_post, wa, wl, wo)


def _rope_tables(n_tokens):
    t = jnp.arange(n_tokens, dtype=jnp.int32)
    row = (t // GRID_W).astype(F32)
    col = (t % GRID_W).astype(F32)
    axis_dim = HEAD_DIM // 2
    inv_freq = ROPE_THETA ** (-jnp.arange(0, axis_dim, 2, dtype=F32) / axis_dim)
    lane = jnp.arange(LANES)
    d = lane % HEAD_DIM
    freq = inv_freq[d % (axis_dim // 2)]
    pos = jnp.where((d < axis_dim)[None, :], row[:, None], col[:, None])
    ang = pos * freq[None, :]
    sign = jnp.where((d % axis_dim) < axis_dim // 2, -1.0, 1.0)
    cos = jnp.cos(ang)
    sin = jnp.sin(ang) * sign[None, :]
    qs = (HEAD_DIM ** -0.5) * math.log2(math.e)
    return cos * qs, sin * qs, cos, sin


def _gate_weights(w_a, w_x, b_a, b_x, wc):
    per = wc // (w_a.shape[-1])
    n_slab = LRU_BLOCKS // per
    eye = jnp.eye(per, dtype=F32)

    def slabs(w):
        w5 = w.reshape(2, n_slab, per, w.shape[-2], w.shape[-1])
        bd = jnp.einsum("dsnij,nm->dsnimj", w5, eye)
        return bd.reshape(2, n_slab, wc, wc)

    wg = jnp.concatenate([slabs(w_a), slabs(w_x)], axis=-1).astype(BF16)
    bg = jnp.concatenate([b_a.reshape(2, n_slab, 1, wc), b_x.reshape(2, n_slab, 1, wc)], axis=-1)
    return wg, bg


def kernel(x, c, ctx, c_ctx, w_mod, b_mod, g_pre, g_post, w_in, lambda_q1, lambda_k1, lambda_q2, lambda_k2,
           g_subln, w_attn_out, conv_w, conv_b, w_rg_a, b_rg_a, w_rg_x, b_rg_x, lru_lambda, w_lru_out, w_out):
    batch, n_lat, d = x.shape
    n_ctx = ctx.shape[1]
    assert w_mod.shape[0] == 1, "single-layer block"
    assert d == HEADS * V_DIM and n_lat % PROJ_TM == 0 and n_ctx % PROJ_TM == 0

    c_all = jnp.zeros((MOD_ROWS, d), F32).at[:batch].set(c).at[batch].set(c_ctx)
    vec = lambda a: a[0].reshape(1, HEAD_DIM)
    mod, lam_row = _modulation(c_all, w_mod[0], b_mod[0].reshape(1, -1),
                               vec(lambda_q1), vec(lambda_k1), vec(lambda_q2), vec(lambda_k2))
    mod3 = mod.reshape(MOD_ROWS, 1, 3 * d)

    w_bf = w_in[0].astype(BF16)
    tables = _rope_tables(n_lat)
    g_pre2 = g_pre[0].reshape(1, d)
    x2 = x.reshape(batch * n_lat, d)
    lat_groups = (("q", d), ("k", d), (None, d), (None, d), (None, d), (None, d), (None, 2 * d))
    lat_groups = tuple((wd, rope) for rope, wd in lat_groups)
    q, k, v, ga, xr, gr, gm = _project(
        x2, mod3, g_pre2, w_bf, tables, lat_groups, n_lat, lambda i, tpb: i // tpb)

    w_ctx = jnp.concatenate([w_bf[:, d:3 * d], w_bf[:, 4 * d:5 * d]], axis=1)
    ctx_groups = ((d, None), (d, None), (d, None))
    kc, vc, xrc = _project(
        ctx.reshape(batch * n_ctx, d), mod3, g_pre2, w_ctx, tables, ctx_groups, n_ctx,
        lambda i, tpb: batch)

    attn = _attention(q, kc, k, vc, v, lam_row, g_subln[0].reshape(1, V_DIM), batch, n_lat, n_ctx)

    wg, bg = _gate_weights(w_rg_a[0], w_rg_x[0], b_rg_a[0], b_rg_x[0], LRU_WC)
    lam = lru_lambda[0].reshape(2, 1, d)
    cb = conv_b[0].reshape(1, d)
    zeros = jnp.zeros((2, batch, d), F32)
    _, _, h_ctx = _rglru(xrc.reshape(batch, n_ctx, d), conv_w[0], cb, wg, bg, lam, zeros)
    hf, hb, _ = _rglru(xr.reshape(batch, n_lat, d), conv_w[0], cb, wg, bg, lam, h_ctx)

    out = _merge(attn, ga, hf.reshape(batch * n_lat, d), hb.reshape(batch * n_lat, d), gr, gm, x2, mod3,
                 g_post[0].reshape(1, d), w_attn_out[0].astype(BF16), w_lru_out[0].astype(BF16),
                 w_out[0].astype(BF16), n_lat)
    return out.reshape(batch, n_lat, d)
```

```python
import functools
import math

import jax
import jax.numpy as jnp
from jax import lax
from jax.experimental import pallas as pl
from jax.experimental.pallas import tpu as pltpu

F32 = jnp.float32
BF16 = jnp.bfloat16

LANES = 128
SUBLANES = 8
BF16_ROWS = 16
VMEM_BYTES = 64 * 1024 * 1024

GRID_W = 64
HEADS = 8
HEAD_DIM = 64
V_DIM = 2 * HEAD_DIM
LRU_BLOCKS = 16
CONV_WIDTH = 4
LRU_C = 8.0
ROPE_THETA = 10000.0
NORM_EPS = 1e-6
LAMBDA_INIT = 0.8 - 0.6 * math.exp(-0.3 * 0)

MOD_ROWS = 24
PROJ_TM = 256
ATTN_ROWS = 128
LRU_TC = 128
LRU_WC = 256
LRU_PITCH = LRU_TC + SUBLANES


def _sigmoid(x):
    return 0.5 * jnp.tanh(0.5 * x) + 0.5


def _vmem_limit(nbytes):
    return int(min(nbytes, VMEM_BYTES - 8 * 1024 * 1024))


def _mod_kernel(c_ref, w_ref, b_ref, lq1_ref, lk1_ref, lq2_ref, lk2_ref, mod_ref, lam_ref):
    c = c_ref[...]
    s = c * _sigmoid(c)
    mod_ref[...] = jnp.dot(s, w_ref[...], precision=lax.Precision.HIGHEST,
                           preferred_element_type=F32) + b_ref[...]
    s1 = jnp.sum(lq1_ref[...] * lk1_ref[...], axis=-1, keepdims=True)
    s2 = jnp.sum(lq2_ref[...] * lk2_ref[...], axis=-1, keepdims=True)
    lam = jnp.exp(s1) - jnp.exp(s2) + LAMBDA_INIT
    lam_ref[...] = jnp.broadcast_to(lam, lam_ref.shape)


def _modulation(c_all, w_mod, b_mod, lq1, lk1, lq2, lk2):
    d = c_all.shape[1]
    n_out = w_mod.shape[1]
    tn = 512
    vec = pl.BlockSpec((1, HEAD_DIM), lambda j: (0, 0))
    return pl.pallas_call(
        _mod_kernel,
        grid=(n_out // tn,),
        in_specs=[
            pl.BlockSpec((MOD_ROWS, d), lambda j: (0, 0)),
            pl.BlockSpec((d, tn), lambda j: (0, j)),
            pl.BlockSpec((1, tn), lambda j: (0, j)),
            vec, vec, vec, vec,
        ],
        out_specs=[
            pl.BlockSpec((MOD_ROWS, tn), lambda j: (0, j)),
            pl.BlockSpec((1, LANES), lambda j: (0, 0)),
        ],
        out_shape=[
            jax.ShapeDtypeStruct((MOD_ROWS, n_out), F32),
            jax.ShapeDtypeStruct((1, LANES), F32),
        ],
        name="mod",
    )(c_all, w_mod, b_mod, lq1, lk1, lq2, lk2)


def _rope_slab(xs, cos, sin, low_half):
    up = pltpu.roll(xs, LANES - 16, 1)
    dn = pltpu.roll(xs, 16, 1)
    return xs * cos + jnp.where(low_half, up, dn) * sin


def _proj_kernel(x_ref, mod_ref, g_ref, w_ref, qcos_ref, qsin_ref, kcos_ref, ksin_ref, *out_refs,
                 d_model, groups):
    x = x_ref[...]
    ms = jnp.mean(x * x, axis=-1, keepdims=True)
    y = x * lax.rsqrt(ms + NORM_EPS) * g_ref[...]
    shift = mod_ref[:, 0:d_model]
    scale = mod_ref[:, d_model:2 * d_model]
    h = (y * (1.0 + scale) + shift).astype(BF16)
    lane = lax.broadcasted_iota(jnp.int32, (x.shape[0], LANES), 1)
    low_half = (lane & 16) == 0
    off = 0
    for (width, rope), o_ref in zip(groups, out_refs):
        r = jnp.dot(h, w_ref[:, off:off + width], preferred_element_type=F32)
        if rope is None:
            o_ref[...] = r.astype(o_ref.dtype)
        else:
            cos_ref, sin_ref = (qcos_ref, qsin_ref) if rope == "q" else (kcos_ref, ksin_ref)
            cos = cos_ref[...]
            sin = sin_ref[...]
            for s in range(width // LANES):
                sl = slice(s * LANES, (s + 1) * LANES)
                o_ref[:, sl] = _rope_slab(r[:, sl], cos, sin, low_half).astype(o_ref.dtype)
        off += width


def _project(x2, mod3, g_pre, w, tables, groups, rows_per_batch, mod_row_of_tile):
    n, d = x2.shape
    tm = PROJ_TM
    tiles_per_batch = rows_per_batch // tm
    n_cols = w.shape[1]
    tab = pl.BlockSpec((tm, LANES), lambda i: (i % tiles_per_batch, 0))
    out_cols = sum(wd for wd, _ in groups)
    est = (d * n_cols * 2 + 2 * tm * d * 4 + 2 * tm * out_cols * 2 + 6 * tm * d * 4
           + 8 * tm * LANES * 4 + (4 << 20))
    return pl.pallas_call(
        functools.partial(_proj_kernel, d_model=d, groups=groups),
        grid=(n // tm,),
        in_specs=[
            pl.BlockSpec((tm, d), lambda i: (i, 0)),
            pl.BlockSpec((None, 1, 3 * d), lambda i: (mod_row_of_tile(i, tiles_per_batch), 0, 0)),
            pl.BlockSpec((1, d), lambda i: (0, 0)),
            pl.BlockSpec((d, n_cols), lambda i: (0, 0), pipeline_mode=pl.Buffered(1)),
            tab, tab, tab, tab,
        ],
        out_specs=[pl.BlockSpec((tm, wd), lambda i: (i, 0)) for wd, _ in groups],
        out_shape=[jax.ShapeDtypeStruct((n, wd), BF16) for wd, _ in groups],
        compiler_params=pltpu.CompilerParams(
            dimension_semantics=("arbitrary",), vmem_limit_bytes=_vmem_limit(est)),
        name="proj",
    )(x2, mod3, g_pre, w, *tables)


def _attn_kernel(q_ref, kc_ref, k_ref, vc_ref, v_ref, lam_ref, g_ref, o_ref, kall_ref, vext_ref,
                 s0_ref, s1_ref, *, n_ctx, rows):
    n_sub = q_ref.shape[0] // rows
    kall_ref[0:n_ctx, :] = kc_ref[...]
    kall_ref[n_ctx:, :] = k_ref[...]
    vext_ref[0:n_ctx, 0:V_DIM] = vc_ref[...]
    vext_ref[n_ctx:, 0:V_DIM] = v_ref[...]
    vext_ref[:, V_DIM:] = jnp.ones((vext_ref.shape[0], V_DIM), BF16)
    lane = lax.broadcasted_iota(jnp.int32, (rows, V_DIM), 1)
    first_half = lane < HEAD_DIM

    def scores(r, s_ref):
        q = q_ref[pl.ds(pl.multiple_of(r * rows, rows), rows), :]
        zero = jnp.zeros_like(q)
        q2 = jnp.concatenate([jnp.where(first_half, q, zero), jnp.where(first_half, zero, q)], axis=0)
        s_ref[...] = lax.dot_general(q2, kall_ref[...], (((1,), (1,)), ((), ())),
                                     preferred_element_type=F32)

    def attend(r, s_ref):
        s = s_ref[...]
        m = jnp.max(s, axis=-1, keepdims=True)
        e = jnp.exp2(s - m).astype(BF16)
        acc = jnp.dot(e, vext_ref[...], preferred_element_type=F32)
        oj = acc[:, 0:V_DIM] / acc[:, V_DIM:]
        o = oj[0:rows] - lam_ref[...] * oj[rows:]
        ms = jnp.mean(o * o, axis=-1, keepdims=True)
        o = o * lax.rsqrt(ms + NORM_EPS) * g_ref[...] * (1.0 - LAMBDA_INIT)
        o_ref[pl.ds(pl.multiple_of(r * rows, rows), rows), :] = o.astype(o_ref.dtype)

    scores(0, s0_ref)

    def pair(p, carry):
        r = 2 * p
        scores(r + 1, s1_ref)
        attend(r, s0_ref)
        scores(r + 2, s0_ref)
        attend(r + 1, s1_ref)
        return carry

    lax.fori_loop(0, n_sub // 2 - 1, pair, 0, unroll=True)
    scores(n_sub - 1, s1_ref)
    attend(n_sub - 2, s0_ref)
    attend(n_sub - 1, s1_ref)


def _attention(q, kc, k, vc, v, lam_row, g_subln, batch, n_lat, n_ctx):
    rows = ATTN_ROWS
    assert n_lat % (2 * rows) == 0
    n_keys = n_ctx + n_lat
    est = (2 * (n_keys * V_DIM * 2 * 2) + n_keys * V_DIM * 2 + n_keys * 2 * V_DIM * 2
           + 4 * n_lat * V_DIM * 2 + 2 * (2 * rows * n_keys * 4) + 4 * (2 * rows * n_keys * 4) + (8 << 20))
    return pl.pallas_call(
        functools.partial(_attn_kernel, n_ctx=n_ctx, rows=rows),
        grid=(batch, HEADS),
        in_specs=[
            pl.BlockSpec((n_lat, V_DIM), lambda b, h: (b, h)),
            pl.BlockSpec((n_ctx, V_DIM), lambda b, h: (b, h)),
            pl.BlockSpec((n_lat, V_DIM), lambda b, h: (b, h)),
            pl.BlockSpec((n_ctx, V_DIM), lambda b, h: (b, h)),
            pl.BlockSpec((n_lat, V_DIM), lambda b, h: (b, h)),
            pl.BlockSpec((1, LANES), lambda b, h: (0, 0)),
            pl.BlockSpec((1, V_DIM), lambda b, h: (0, 0)),
        ],
        out_specs=pl.BlockSpec((n_lat, V_DIM), lambda b, h: (b, h)),
        out_shape=jax.ShapeDtypeStruct((batch * n_lat, HEADS * V_DIM), BF16),
        scratch_shapes=[
            pltpu.VMEM((n_keys, V_DIM), BF16),
            pltpu.VMEM((n_keys, 2 * V_DIM), BF16),
            pltpu.VMEM((2 * rows, n_keys), F32),
            pltpu.VMEM((2 * rows, n_keys), F32),
        ],
        compiler_params=pltpu.CompilerParams(
            dimension_semantics=("arbitrary", "arbitrary"),
            vmem_limit_bytes=_vmem_limit(est)),
        name="attn",
    )(q, kc, k, vc, v, lam_row, g_subln)


def _lru_kernel(xf_ref, xfp_ref, xfn_ref, xb_ref, xbp_ref, xbn_ref, cw_ref, cb_ref, wg_ref, bg_ref,
                lam_ref, h0_ref, hf_ref, hb_ref, hfin_ref, xt_ref, xc_ref, a_ref, u_ref, ht_ref, st_ref,
                *, batch, tc, wc):
    i = pl.program_id(1)
    n = pl.num_programs(1)
    n_slab = wc // LANES
    n_grp = batch // SUBLANES
    pitch = LRU_PITCH
    gate_rows = 512
    cols = [slice(l * LANES, (l + 1) * LANES) for l in range(n_slab)]

    @pl.when(i == 0)
    def _():
        st_ref[...] = h0_ref[...]

    def stage_input(d, cur_ref, prev_ref, next_ref, chunk):
        sub = lax.broadcasted_iota(jnp.int32, (SUBLANES, LANES), 0)
        for l in range(n_slab):
            for b in range(batch):
                xt_ref[d, l, b * pitch + SUBLANES:b * pitch + SUBLANES + tc, :] = (
                    cur_ref[b, :, cols[l]].astype(F32))
            for kb in range(batch + 1):
                if kb < batch:
                    prev = prev_ref[kb, :, cols[l]].astype(F32)[BF16_ROWS - SUBLANES:, :]
                    prev = jnp.where(chunk == 0, 0.0, prev)
                else:
                    prev = jnp.zeros((SUBLANES, LANES), F32)
                if kb > 0:
                    nxt = next_ref[kb - 1, :, cols[l]].astype(F32)[0:SUBLANES, :]
                    nxt = jnp.where(chunk == n - 1, 0.0, nxt)
                else:
                    nxt = jnp.zeros((SUBLANES, LANES), F32)
                xt_ref[d, l, kb * pitch:kb * pitch + SUBLANES, :] = jnp.where(sub < 2, nxt, prev)

    stage_input(0, xf_ref, xfp_ref, xfn_ref, i)
    stage_input(1, xb_ref, xbp_ref, xbn_ref, n - 1 - i)

    def x_rows(d, l, g, tau):
        start = g * SUBLANES * pitch + SUBLANES + tau
        return xt_ref[d, l, pl.ds(start, SUBLANES, stride=pitch), :]

    taps = [[jnp.broadcast_to(cw_ref[j:j + 1, cols[l]], (SUBLANES, LANES)) for j in range(CONV_WIDTH)]
            for l in range(n_slab)]
    bias = [jnp.broadcast_to(cb_ref[:, cols[l]], (SUBLANES, LANES)) for l in range(n_slab)]
    units = [(d, l, g) for d in range(2) for l in range(n_slab) for g in range(n_grp)]

    def conv_step(t, win):
        new = []
        out = {}
        for (d, l, g), (xm1, x0, xp1) in zip(units, win):
            xp2 = x_rows(d, l, g, t + 2)
            w = taps[l]
            out[d, l, g] = bias[l] + xm1 * w[0] + x0 * w[1] + xp1 * w[2] + xp2 * w[3]
            new.append((x0, xp1, xp2))
        row = pl.ds(pl.multiple_of(t * batch, batch), batch)
        for d in range(2):
            xc_ref[d, row, :] = jnp.concatenate(
                [jnp.concatenate([out[d, l, g] for l in range(n_slab)], axis=1) for g in range(n_grp)], axis=0)
        return new

    lax.fori_loop(0, tc, conv_step,
                  [tuple(x_rows(d, l, g, tau) for tau in (-1, 0, 1)) for d, l, g in units], unroll=4)

    for d in range(2):
        half_cl = (-0.5 * LRU_C * math.log2(math.e)) * jax.nn.softplus(-lam_ref[d])
        for r0 in range(0, tc * batch, gate_rows):
            rows = slice(r0, r0 + gate_rows)
            xc = xc_ref[d, rows, :]
            g = jnp.dot(xc.astype(BF16), wg_ref[d], preferred_element_type=F32) + bg_ref[d]
            a = jnp.exp2(half_cl * jnp.tanh(0.5 * g[:, 0:wc]) + half_cl)
            a_ref[d, rows, :] = a
            y = 1.0 - a * a
            mult = y * lax.rsqrt(jnp.maximum(y, 1e-30))
            u_ref[d, rows, :] = (mult * xc) * _sigmoid(g[:, wc:])

    def scan_step(t, hs):
        times = (t, tc - 1 - t)
        new = []
        for d in range(2):
            row = pl.ds(pl.multiple_of(times[d] * batch, batch), batch)
            h = a_ref[d, row, :] * hs[d] + u_ref[d, row, :]
            for l in range(n_slab):
                for g in range(n_grp):
                    dst = pl.ds(g * SUBLANES * pitch + times[d], SUBLANES, stride=pitch)
                    ht_ref[d, l, dst, :] = h[g * SUBLANES:(g + 1) * SUBLANES, cols[l]]
            new.append(h)
        return new

    fin = lax.fori_loop(0, tc, scan_step, [st_ref[0], st_ref[1]], unroll=8)
    st_ref[0] = fin[0]
    st_ref[1] = fin[1]

    for d, o_ref in ((0, hf_ref), (1, hb_ref)):
        for b in range(batch):
            for l in range(n_slab):
                o_ref[b, :, cols[l]] = ht_ref[d, l, b * pitch:b * pitch + tc, :].astype(o_ref.dtype)

    @pl.when(i == n - 1)
    def _():
        hfin_ref[...] = st_ref[...]


def _rglru(xr3, conv_w, conv_b, wg, bg, lam, h0):
    batch, length, width = xr3.shape
    tc, wc = LRU_TC, LRU_WC
    n = length // tc
    hb_per = tc // BF16_ROWS
    n_halo = length // BF16_ROWS
    n_slab = wc // LANES

    def cur(fwd):
        return pl.BlockSpec((batch, tc, wc), lambda s, i: (0, i if fwd else n - 1 - i, s))

    def prev(fwd):
        return pl.BlockSpec((batch, BF16_ROWS, wc), lambda s, i: (
            0, jnp.maximum((i if fwd else n - 1 - i) * hb_per - 1, 0), s))

    def nxt(fwd):
        return pl.BlockSpec((batch, BF16_ROWS, wc), lambda s, i: (
            0, jnp.minimum(((i if fwd else n - 1 - i) + 1) * hb_per, n_halo - 1), s))

    pitched_rows = batch * LRU_PITCH + SUBLANES
    time_major = pltpu.VMEM((2, tc * batch, wc), F32)
    est = (2 * 2 * n_slab * pitched_rows * LANES * 4 + 3 * 2 * tc * batch * wc * 4
           + 8 * batch * tc * wc * 2 + 8 * 512 * 2 * wc * 4 + (8 << 20))
    return pl.pallas_call(
        functools.partial(_lru_kernel, batch=batch, tc=tc, wc=wc),
        grid=(width // wc, n),
        in_specs=[
            cur(True), prev(True), nxt(True), cur(False), prev(False), nxt(False),
            pl.BlockSpec((CONV_WIDTH, wc), lambda s, i: (0, s)),
            pl.BlockSpec((1, wc), lambda s, i: (0, s)),
            pl.BlockSpec((2, None, wc, 2 * wc), lambda s, i: (0, s, 0, 0)),
            pl.BlockSpec((2, None, 1, 2 * wc), lambda s, i: (0, s, 0, 0)),
            pl.BlockSpec((2, 1, wc), lambda s, i: (0, 0, s)),
            pl.BlockSpec((2, batch, wc), lambda s, i: (0, 0, s)),
        ],
        out_specs=[
            pl.BlockSpec((batch, tc, wc), lambda s, i: (0, i, s)),
            pl.BlockSpec((batch, tc, wc), lambda s, i: (0, n - 1 - i, s)),
            pl.BlockSpec((2, batch, wc), lambda s, i: (0, 0, s)),
        ],
        out_shape=[
            jax.ShapeDtypeStruct((batch, length, width), BF16),
            jax.ShapeDtypeStruct((batch, length, width), BF16),
            jax.ShapeDtypeStruct((2, batch, width), F32),
        ],
        scratch_shapes=[
            pltpu.VMEM((2, n_slab, pitched_rows, LANES), F32),
            time_major, time_major, time_major,
            pltpu.VMEM((2, n_slab, pitched_rows, LANES), F32),
            pltpu.VMEM((2, batch, wc), F32),
        ],
        compiler_params=pltpu.CompilerParams(
            dimension_semantics=("arbitrary", "arbitrary"), vmem_limit_bytes=_vmem_limit(est)),
        name="lru",
    )(xr3, xr3, xr3, xr3, xr3, xr3, conv_w, conv_b, wg, bg, lam, h0)


def _merge_kernel(attn_ref, ga_ref, hf_ref, hb_ref, gr_ref, gm_ref, x_ref, mod_ref, gp_ref,
                  wa_ref, wl_ref, wo_ref, o_ref, *, d_model):
    def silu(ref):
        g = ref[...].astype(F32)
        return g * _sigmoid(g)

    a_in = (attn_ref[...].astype(F32) * silu(ga_ref)).astype(BF16)
    y_attn = jnp.dot(a_in, wa_ref[...], preferred_element_type=F32)
    lru = hf_ref[...].astype(F32) + hb_ref[...].astype(F32)
    l_in = (lru * silu(gr_ref)).astype(BF16)
    y_lru = jnp.dot(l_in, wl_ref[...], preferred_element_type=F32)
    m_attn = _sigmoid(gm_ref[:, 0:d_model].astype(F32))
    m_lru = _sigmoid(gm_ref[:, d_model:].astype(F32))
    z = (m_attn * y_attn + m_lru * y_lru).astype(BF16)
    y = jnp.dot(z, wo_ref[...], preferred_element_type=F32)
    ms = jnp.mean(y * y, axis=-1, keepdims=True)
    yn = y * lax.rsqrt(ms + NORM_EPS) * gp_ref[...]
    gate = mod_ref[:, 2 * d_model:]
    o_ref[...] = x_ref[...] + gate * yn


def _merge(attn, ga, hf, hb, gr, gm, x2, mod3, g_post, wa, wl, wo, rows_per_batch):
    n, d = x2.shape
    tm = PROJ_TM
    tiles_per_batch = rows_per_batch // tm
    row = pl.BlockSpec((tm, d), lambda i: (i, 0))
    wspec = pl.BlockSpec((d, d), lambda i: (0, 0), pipeline_mode=pl.Buffered(1))
    est = 3 * d * d * 2 + 2 * (5 * tm * d * 2 + tm * 2 * d * 2 + 2 * tm * d * 4) + 10 * tm * d * 4 + (4 << 20)
    return pl.pallas_call(
        functools.partial(_merge_kernel, d_model=d),
        grid=(n // tm,),
        in_specs=[
            row, row, row, row, row,
            pl.BlockSpec((tm, 2 * d), lambda i: (i, 0)),
            row,
            pl.BlockSpec((None, 1, 3 * d), lambda i: (i // tiles_per_batch, 0, 0)),
            pl.BlockSpec((1, d), lambda i: (0, 0)),
            wspec, wspec, wspec,
        ],
        out_specs=row,
        out_shape=jax.ShapeDtypeStruct((n, d), F32),
        compiler_params=pltpu.CompilerParams(
            dimension_semantics=("arbitrary",), vmem_limit_bytes=_vmem_limit(est)),
        name="merge",
    )(attn, ga, hf, hb, gr, gm, x2, mod3, g_post, wa, wl, wo)


def _rope_tables(n_tokens):
    t = jnp.arange(n_tokens, dtype=jnp.int32)
    row = (t // GRID_W).astype(F32)
    col = (t % GRID_W).astype(F32)
    axis_dim = HEAD_DIM // 2
    inv_freq = ROPE_THETA ** (-jnp.arange(0, axis_dim, 2, dtype=F32) / axis_dim)
    lane = jnp.arange(LANES)
    d = lane % HEAD_DIM
    freq = inv_freq[d % (axis_dim // 2)]
    pos = jnp.where((d < axis_dim)[None, :], row[:, None], col[:, None])
    ang = pos * freq[None, :]
    sign = jnp.where((d % axis_dim) < axis_dim // 2, -1.0, 1.0)
    cos = jnp.cos(ang)
    sin = jnp.sin(ang) * sign[None, :]
    qs = (HEAD_DIM ** -0.5) * math.log2(math.e)
    return cos * qs, sin * qs, cos, sin


def _gate_weights(w_a, w_x, b_a, b_x, wc):
    per = wc // (w_a.shape[-1])
    n_slab = LRU_BLOCKS // per
    eye = jnp.eye(per, dtype=F32)

    def slabs(w):
        w5 = w.reshape(2, n_slab, per, w.shape[-2], w.shape[-1])
        bd = jnp.einsum("dsnij,nm->dsnimj", w5, eye)
        return bd.reshape(2, n_slab, wc, wc)

    wg = jnp.concatenate([slabs(w_a), slabs(w_x)], axis=-1).astype(BF16)
    bg = jnp.concatenate([b_a.reshape(2, n_slab, 1, wc), b_x.reshape(2, n_slab, 1, wc)], axis=-1)
    return wg, bg


def kernel(x, c, ctx, c_ctx, w_mod, b_mod, g_pre, g_post, w_in, lambda_q1, lambda_k1, lambda_q2, lambda_k2,
           g_subln, w_attn_out, conv_w, conv_b, w_rg_a, b_rg_a, w_rg_x, b_rg_x, lru_lambda, w_lru_out, w_out):
    batch, n_lat, d = x.shape
    n_ctx = ctx.shape[1]
    assert w_mod.shape[0] == 1, "single-layer block"
    assert d == HEADS * V_DIM and n_lat % PROJ_TM == 0 and n_ctx % PROJ_TM == 0

    c_all = jnp.zeros((MOD_ROWS, d), F32).at[:batch].set(c).at[batch].set(c_ctx)
    vec = lambda a: a[0].reshape(1, HEAD_DIM)
    mod, lam_row = _modulation(c_all, w_mod[0], b_mod[0].reshape(1, -1),
                               vec(lambda_q1), vec(lambda_k1), vec(lambda_q2), vec(lambda_k2))
    mod3 = mod.reshape(MOD_ROWS, 1, 3 * d)

    w_bf = w_in[0].astype(BF16)
    tables = _rope_tables(n_lat)
    g_pre2 = g_pre[0].reshape(1, d)
    x2 = x.reshape(batch * n_lat, d)
    lat_groups = (("q", d), ("k", d), (None, d), (None, d), (None, d), (None, d), (None, 2 * d))
    lat_groups = tuple((wd, rope) for rope, wd in lat_groups)
    q, k, v, ga, xr, gr, gm = _project(
        x2, mod3, g_pre2, w_bf, tables, lat_groups, n_lat, lambda i, tpb: i // tpb)

    w_ctx = jnp.concatenate([w_bf[:, d:3 * d], w_bf[:, 4 * d:5 * d]], axis=1)
    ctx_groups = ((d, None), (d, None), (d, None))
    kc, vc, xrc = _project(
        ctx.reshape(batch * n_ctx, d), mod3, g_pre2, w_ctx, tables, ctx_groups, n_ctx,
        lambda i, tpb: batch)

    attn = _attention(q, kc, k, vc, v, lam_row, g_subln[0].reshape(1, V_DIM), batch, n_lat, n_ctx)

    wg, bg = _gate_weights(w_rg_a[0], w_rg_x[0], b_rg_a[0], b_rg_x[0], LRU_WC)
    lam = lru_lambda[0].reshape(2, 1, d)
    cb = conv_b[0].reshape(1, d)
    zeros = jnp.zeros((2, batch, d), F32)
    _, _, h_ctx = _rglru(xrc.reshape(batch, n_ctx, d), conv_w[0], cb, wg, bg, lam, zeros)
    hf, hb, _ = _rglru(xr.reshape(batch, n_lat, d), conv_w[0], cb, wg, bg, lam, h_ctx)

    out = _merge(attn, ga, hf.reshape(batch * n_lat, d), hb.reshape(batch * n_lat, d), gr, gm, x2, mod3,
                 g_post[0].reshape(1, d), w_attn_out[0].astype(BF16), w_lru_out[0].astype(BF16),
                 w_out[0].astype(BF16), n_lat)
    return out.reshape(batch, n_lat, d)
```

```python
import functools
import math

import jax
import jax.numpy as jnp
import numpy as np
from jax import lax
from jax.experimental import pallas as pl
from jax.experimental.pallas import tpu as pltpu

F32 = jnp.float32
BF16 = jnp.bfloat16

LANES = 128
SUBLANES = 8
BF16_ROWS = 16
VMEM_BYTES = 64 * 1024 * 1024

GRID_W = 64
HEADS = 8
HEAD_DIM = 64
V_DIM = 2 * HEAD_DIM
LRU_BLOCKS = 16
CONV_WIDTH = 4
LRU_C = 8.0
ROPE_THETA = 10000.0
NORM_EPS = 1e-6
LAMBDA_INIT = 0.8 - 0.6 * math.exp(-0.3 * 0)

MOD_ROWS = 24
PROJ_TM = 512
MERGE_TM = 512
ATTN_ROWS = 128
LRU_TC = 128
LRU_WC = 256
LRU_PITCH = LRU_TC + SUBLANES


def _sigmoid(x):
    return 0.5 * jnp.tanh(0.5 * x) + 0.5


def _vmem_limit(nbytes):
    return int(min(nbytes, VMEM_BYTES - 8 * 1024 * 1024))


def _mod_kernel(c_ref, w_ref, b_ref, lq1_ref, lk1_ref, lq2_ref, lk2_ref, mod_ref, lam_ref):
    c = c_ref[...]
    s = c * _sigmoid(c)
    mod_ref[...] = jnp.dot(s, w_ref[...], precision=lax.Precision.HIGHEST,
                           preferred_element_type=F32) + b_ref[...]
    s1 = jnp.sum(lq1_ref[...] * lk1_ref[...], axis=-1, keepdims=True)
    s2 = jnp.sum(lq2_ref[...] * lk2_ref[...], axis=-1, keepdims=True)
    lam = jnp.exp(s1) - jnp.exp(s2) + LAMBDA_INIT
    lam_ref[...] = jnp.broadcast_to(lam, lam_ref.shape)


def _modulation(c_all, w_mod, b_mod, lq1, lk1, lq2, lk2):
    d = c_all.shape[1]
    n_out = w_mod.shape[1]
    tn = 512
    vec = pl.BlockSpec((1, HEAD_DIM), lambda j: (0, 0))
    return pl.pallas_call(
        _mod_kernel,
        grid=(n_out // tn,),
        in_specs=[
            pl.BlockSpec((MOD_ROWS, d), lambda j: (0, 0)),
            pl.BlockSpec((d, tn), lambda j: (0, j)),
            pl.BlockSpec((1, tn), lambda j: (0, j)),
            vec, vec, vec, vec,
        ],
        out_specs=[
            pl.BlockSpec((MOD_ROWS, tn), lambda j: (0, j)),
            pl.BlockSpec((1, LANES), lambda j: (0, 0)),
        ],
        out_shape=[
            jax.ShapeDtypeStruct((MOD_ROWS, n_out), F32),
            jax.ShapeDtypeStruct((1, LANES), F32),
        ],
        name="mod",
    )(c_all, w_mod, b_mod, lq1, lk1, lq2, lk2)


def _rope_slab(xs, cos, sin, low_half):
    up = pltpu.roll(xs, LANES - 16, 1)
    dn = pltpu.roll(xs, 16, 1)
    return xs * cos + jnp.where(low_half, up, dn) * sin


def _proj_kernel(x_ref, mod_ref, g_ref, w_ref, qcos_ref, qsin_ref, kcos_ref, ksin_ref, *out_refs,
                 d_model, groups):
    x = x_ref[...]
    ms = jnp.mean(x * x, axis=-1, keepdims=True)
    y = x * lax.rsqrt(ms + NORM_EPS) * g_ref[...]
    shift = mod_ref[:, 0:d_model]
    scale = mod_ref[:, d_model:2 * d_model]
    h = (y * (1.0 + scale) + shift).astype(BF16)
    lane = lax.broadcasted_iota(jnp.int32, (x.shape[0], LANES), 1)
    low_half = (lane & 16) == 0
    off = 0
    for (width, rope), o_ref in zip(groups, out_refs):
        r = jnp.dot(h, w_ref[:, off:off + width], preferred_element_type=F32)
        if rope is None:
            o_ref[...] = r.astype(o_ref.dtype)
        else:
            cos_ref, sin_ref = (qcos_ref, qsin_ref) if rope == "q" else (kcos_ref, ksin_ref)
            cos = cos_ref[...]
            sin = sin_ref[...]
            for s in range(width // LANES):
                sl = slice(s * LANES, (s + 1) * LANES)
                o_ref[:, sl] = _rope_slab(r[:, sl], cos, sin, low_half).astype(o_ref.dtype)
        off += width


def _project(x2, mod3, g_pre, w, tables, groups, rows_per_batch, mod_row_of_tile):
    n, d = x2.shape
    tm = min(PROJ_TM, rows_per_batch)
    tiles_per_batch = rows_per_batch // tm
    n_cols = w.shape[1]
    tab = pl.BlockSpec((tm, LANES), lambda i: (i % tiles_per_batch, 0))
    out_cols = sum(wd for wd, _ in groups)
    est = (d * n_cols * 2 + 2 * tm * d * 4 + 2 * tm * out_cols * 2 + 6 * tm * d * 4
           + 8 * tm * LANES * 4 + (4 << 20))
    return pl.pallas_call(
        functools.partial(_proj_kernel, d_model=d, groups=groups),
        grid=(n // tm,),
        in_specs=[
            pl.BlockSpec((tm, d), lambda i: (i, 0)),
            pl.BlockSpec((None, 1, 3 * d), lambda i: (mod_row_of_tile(i, tiles_per_batch), 0, 0)),
            pl.BlockSpec((1, d), lambda i: (0, 0)),
            pl.BlockSpec((d, n_cols), lambda i: (0, 0), pipeline_mode=pl.Buffered(1)),
            tab, tab, tab, tab,
        ],
        out_specs=[pl.BlockSpec((tm, wd), lambda i: (i, 0)) for wd, _ in groups],
        out_shape=[jax.ShapeDtypeStruct((n, wd), BF16) for wd, _ in groups],
        compiler_params=pltpu.CompilerParams(
            dimension_semantics=("arbitrary",), vmem_limit_bytes=_vmem_limit(est)),
        name="proj",
    )(x2, mod3, g_pre, w, *tables)


def _attn_kernel(q_ref, kc_ref, k_ref, vc_ref, v_ref, lam_ref, g_ref, o_ref, kall_ref, vext_ref,
                 s0_ref, s1_ref, *, n_ctx, rows):
    n_sub = q_ref.shape[0] // rows
    kall_ref[0:n_ctx, :] = kc_ref[...]
    kall_ref[n_ctx:, :] = k_ref[...]
    vext_ref[0:n_ctx, 0:V_DIM] = vc_ref[...]
    vext_ref[n_ctx:, 0:V_DIM] = v_ref[...]
    vext_ref[:, V_DIM:] = jnp.ones((vext_ref.shape[0], V_DIM), BF16)
    lane = lax.broadcasted_iota(jnp.int32, (rows, V_DIM), 1)
    first_half = lane < HEAD_DIM

    def scores(r, s_ref):
        q = q_ref[pl.ds(pl.multiple_of(r * rows, rows), rows), :]
        zero = jnp.zeros_like(q)
        q2 = jnp.concatenate([jnp.where(first_half, q, zero), jnp.where(first_half, zero, q)], axis=0)
        s_ref[...] = lax.dot_general(q2, kall_ref[...], (((1,), (1,)), ((), ())),
                                     preferred_element_type=F32)

    def attend(r, s_ref):
        s = s_ref[...]
        m = jnp.max(s, axis=-1, keepdims=True)
        e = jnp.exp2(s - m).astype(BF16)
        acc = jnp.dot(e, vext_ref[...], preferred_element_type=F32)
        oj = acc[:, 0:V_DIM] / acc[:, V_DIM:]
        o = oj[0:rows] - lam_ref[...] * oj[rows:]
        ms = jnp.mean(o * o, axis=-1, keepdims=True)
        o = o * lax.rsqrt(ms + NORM_EPS) * g_ref[...] * (1.0 - LAMBDA_INIT)
        o_ref[pl.ds(pl.multiple_of(r * rows, rows), rows), :] = o.astype(o_ref.dtype)

    scores(0, s0_ref)

    def pair(p, carry):
        r = 2 * p
        scores(r + 1, s1_ref)
        attend(r, s0_ref)
        scores(r + 2, s0_ref)
        attend(r + 1, s1_ref)
        return carry

    lax.fori_loop(0, n_sub // 2 - 1, pair, 0, unroll=True)
    scores(n_sub - 1, s1_ref)
    attend(n_sub - 2, s0_ref)
    attend(n_sub - 1, s1_ref)


def _attention(q, kc, k, vc, v, lam_row, g_subln, batch, n_lat, n_ctx):
    rows = ATTN_ROWS
    assert n_lat % (2 * rows) == 0
    n_keys = n_ctx + n_lat
    est = (2 * (n_keys * V_DIM * 2 * 2) + n_keys * V_DIM * 2 + n_keys * 2 * V_DIM * 2
           + 4 * n_lat * V_DIM * 2 + 2 * (2 * rows * n_keys * 4) + 4 * (2 * rows * n_keys * 4) + (8 << 20))
    return pl.pallas_call(
        functools.partial(_attn_kernel, n_ctx=n_ctx, rows=rows),
        grid=(batch, HEADS),
        in_specs=[
            pl.BlockSpec((n_lat, V_DIM), lambda b, h: (b, h)),
            pl.BlockSpec((n_ctx, V_DIM), lambda b, h: (b, h)),
            pl.BlockSpec((n_lat, V_DIM), lambda b, h: (b, h)),
            pl.BlockSpec((n_ctx, V_DIM), lambda b, h: (b, h)),
            pl.BlockSpec((n_lat, V_DIM), lambda b, h: (b, h)),
            pl.BlockSpec((1, LANES), lambda b, h: (0, 0)),
            pl.BlockSpec((1, V_DIM), lambda b, h: (0, 0)),
        ],
        out_specs=pl.BlockSpec((n_lat, V_DIM), lambda b, h: (b, h)),
        out_shape=jax.ShapeDtypeStruct((batch * n_lat, HEADS * V_DIM), BF16),
        scratch_shapes=[
            pltpu.VMEM((n_keys, V_DIM), BF16),
            pltpu.VMEM((n_keys, 2 * V_DIM), BF16),
            pltpu.VMEM((2 * rows, n_keys), F32),
            pltpu.VMEM((2 * rows, n_keys), F32),
        ],
        compiler_params=pltpu.CompilerParams(
            dimension_semantics=("arbitrary", "arbitrary"),
            vmem_limit_bytes=_vmem_limit(est)),
        name="attn",
    )(q, kc, k, vc, v, lam_row, g_subln)


def _lru_kernel(xf_ref, xfp_ref, xfn_ref, xb_ref, xbp_ref, xbn_ref, cw_ref, cb_ref, wg_ref, bg_ref,
                lam_ref, h0_ref, hf_ref, hb_ref, hfin_ref, xt_ref, xc_ref, a_ref, u_ref, ht_ref, st_ref,
                *, batch, tc, wc):
    i = pl.program_id(1)
    n = pl.num_programs(1)
    n_slab = wc // LANES
    n_grp = batch // SUBLANES
    pitch = LRU_PITCH
    gate_rows = 512
    cols = [slice(l * LANES, (l + 1) * LANES) for l in range(n_slab)]

    @pl.when(i == 0)
    def _():
        st_ref[...] = h0_ref[...]

    def stage_input(d, cur_ref, prev_ref, next_ref, chunk):
        sub = lax.broadcasted_iota(jnp.int32, (SUBLANES, LANES), 0)
        for l in range(n_slab):
            for b in range(batch):
                xt_ref[d, l, b * pitch + SUBLANES:b * pitch + SUBLANES + tc, :] = (
                    cur_ref[b, :, cols[l]].astype(F32))
            for kb in range(batch + 1):
                if kb < batch:
                    prev = prev_ref[kb, :, cols[l]].astype(F32)[BF16_ROWS - SUBLANES:, :]
                    prev = jnp.where(chunk == 0, 0.0, prev)
                else:
                    prev = jnp.zeros((SUBLANES, LANES), F32)
                if kb > 0:
                    nxt = next_ref[kb - 1, :, cols[l]].astype(F32)[0:SUBLANES, :]
                    nxt = jnp.where(chunk == n - 1, 0.0, nxt)
                else:
                    nxt = jnp.zeros((SUBLANES, LANES), F32)
                xt_ref[d, l, kb * pitch:kb * pitch + SUBLANES, :] = jnp.where(sub < 2, nxt, prev)

    stage_input(0, xf_ref, xfp_ref, xfn_ref, i)
    stage_input(1, xb_ref, xbp_ref, xbn_ref, n - 1 - i)

    def x_rows(d, l, g, tau):
        start = g * SUBLANES * pitch + SUBLANES + tau
        return xt_ref[d, l, pl.ds(start, SUBLANES, stride=pitch), :]

    taps = [[jnp.broadcast_to(cw_ref[j:j + 1, cols[l]], (SUBLANES, LANES)) for j in range(CONV_WIDTH)]
            for l in range(n_slab)]
    bias = [jnp.broadcast_to(cb_ref[:, cols[l]], (SUBLANES, LANES)) for l in range(n_slab)]
    units = [(d, l, g) for d in range(2) for l in range(n_slab) for g in range(n_grp)]

    def conv_step(t, win):
        new = []
        out = {}
        for (d, l, g), (xm1, x0, xp1) in zip(units, win):
            xp2 = x_rows(d, l, g, t + 2)
            w = taps[l]
            out[d, l, g] = bias[l] + xm1 * w[0] + x0 * w[1] + xp1 * w[2] + xp2 * w[3]
            new.append((x0, xp1, xp2))
        row = pl.ds(pl.multiple_of(t * batch, batch), batch)
        for d in range(2):
            xc_ref[d, row, :] = jnp.concatenate(
                [jnp.concatenate([out[d, l, g] for l in range(n_slab)], axis=1) for g in range(n_grp)], axis=0)
        return new

    lax.fori_loop(0, tc, conv_step,
                  [tuple(x_rows(d, l, g, tau) for tau in (-1, 0, 1)) for d, l, g in units], unroll=4)

    for d in range(2):
        half_cl = (-0.5 * LRU_C * math.log2(math.e)) * jax.nn.softplus(-lam_ref[d])
        for r0 in range(0, tc * batch, gate_rows):
            rows = slice(r0, r0 + gate_rows)
            xc = xc_ref[d, rows, :]
            g = jnp.dot(xc.astype(BF16), wg_ref[d], preferred_element_type=F32) + bg_ref[d]
            a = jnp.exp2(half_cl * jnp.tanh(0.5 * g[:, 0:wc]) + half_cl)
            a_ref[d, rows, :] = a
            y = 1.0 - a * a
            mult = y * lax.rsqrt(jnp.maximum(y, 1e-30))
            u_ref[d, rows, :] = (mult * xc) * _sigmoid(g[:, wc:])

    def scan_step(t, hs):
        times = (t, tc - 1 - t)
        new = []
        for d in range(2):
            row = pl.ds(pl.multiple_of(times[d] * batch, batch), batch)
            h = a_ref[d, row, :] * hs[d] + u_ref[d, row, :]
            for l in range(n_slab):
                for g in range(n_grp):
                    dst = pl.ds(g * SUBLANES * pitch + times[d], SUBLANES, stride=pitch)
                    ht_ref[d, l, dst, :] = h[g * SUBLANES:(g + 1) * SUBLANES, cols[l]]
            new.append(h)
        return new

    fin = lax.fori_loop(0, tc, scan_step, [st_ref[0], st_ref[1]], unroll=8)
    st_ref[0] = fin[0]
    st_ref[1] = fin[1]

    for d, o_ref in ((0, hf_ref), (1, hb_ref)):
        for b in range(batch):
            for l in range(n_slab):
                o_ref[b, :, cols[l]] = ht_ref[d, l, b * pitch:b * pitch + tc, :].astype(o_ref.dtype)

    @pl.when(i == n - 1)
    def _():
        hfin_ref[...] = st_ref[...]


def _rglru(xr3, conv_w, conv_b, wg, bg, lam, h0):
    batch, length, width = xr3.shape
    tc, wc = LRU_TC, LRU_WC
    n = length // tc
    hb_per = tc // BF16_ROWS
    n_halo = length // BF16_ROWS
    n_slab = wc // LANES

    def cur(fwd):
        return pl.BlockSpec((batch, tc, wc), lambda s, i: (0, i if fwd else n - 1 - i, s))

    def prev(fwd):
        return pl.BlockSpec((batch, BF16_ROWS, wc), lambda s, i: (
            0, jnp.maximum((i if fwd else n - 1 - i) * hb_per - 1, 0), s))

    def nxt(fwd):
        return pl.BlockSpec((batch, BF16_ROWS, wc), lambda s, i: (
            0, jnp.minimum(((i if fwd else n - 1 - i) + 1) * hb_per, n_halo - 1), s))

    pitched_rows = batch * LRU_PITCH + SUBLANES
    time_major = pltpu.VMEM((2, tc * batch, wc), F32)
    est = (2 * 2 * n_slab * pitched_rows * LANES * 4 + 3 * 2 * tc * batch * wc * 4
           + 8 * batch * tc * wc * 2 + 8 * 512 * 2 * wc * 4 + (8 << 20))
    return pl.pallas_call(
        functools.partial(_lru_kernel, batch=batch, tc=tc, wc=wc),
        grid=(width // wc, n),
        in_specs=[
            cur(True), prev(True), nxt(True), cur(False), prev(False), nxt(False),
            pl.BlockSpec((CONV_WIDTH, wc), lambda s, i: (0, s)),
            pl.BlockSpec((1, wc), lambda s, i: (0, s)),
            pl.BlockSpec((2, None, wc, 2 * wc), lambda s, i: (0, s, 0, 0)),
            pl.BlockSpec((2, None, 1, 2 * wc), lambda s, i: (0, s, 0, 0)),
            pl.BlockSpec((2, 1, wc), lambda s, i: (0, 0, s)),
            pl.BlockSpec((2, batch, wc), lambda s, i: (0, 0, s)),
        ],
        out_specs=[
            pl.BlockSpec((batch, tc, wc), lambda s, i: (0, i, s)),
            pl.BlockSpec((batch, tc, wc), lambda s, i: (0, n - 1 - i, s)),
            pl.BlockSpec((2, batch, wc), lambda s, i: (0, 0, s)),
        ],
        out_shape=[
            jax.ShapeDtypeStruct((batch, length, width), BF16),
            jax.ShapeDtypeStruct((batch, length, width), BF16),
            jax.ShapeDtypeStruct((2, batch, width), F32),
        ],
        scratch_shapes=[
            pltpu.VMEM((2, n_slab, pitched_rows, LANES), F32),
            time_major, time_major, time_major,
            pltpu.VMEM((2, n_slab, pitched_rows, LANES), F32),
            pltpu.VMEM((2, batch, wc), F32),
        ],
        compiler_params=pltpu.CompilerParams(
            dimension_semantics=("arbitrary", "arbitrary"), vmem_limit_bytes=_vmem_limit(est)),
        name="lru",
    )(xr3, xr3, xr3, xr3, xr3, xr3, conv_w, conv_b, wg, bg, lam, h0)


def _merge_kernel(attn_ref, ga_ref, hf_ref, hb_ref, gr_ref, gm_ref, x_ref, mod_ref, gp_ref,
                  wa_ref, wl_ref, wo_ref, o_ref, *, d_model):
    def silu(ref):
        h = ref[...] * 0.5
        return h + h * jnp.tanh(h)

    a_in = attn_ref[...] * silu(ga_ref)
    y_attn = jnp.dot(a_in, wa_ref[...], preferred_element_type=F32)
    l_in = (hf_ref[...] + hb_ref[...]) * silu(gr_ref)
    y_lru = jnp.dot(l_in, wl_ref[...], preferred_element_type=F32)
    t_attn = jnp.tanh(0.5 * gm_ref[:, 0:d_model].astype(F32))
    t_lru = jnp.tanh(0.5 * gm_ref[:, d_model:].astype(F32))
    z = (0.5 * ((t_attn + 1.0) * y_attn + (t_lru + 1.0) * y_lru)).astype(BF16)
    y = jnp.dot(z, wo_ref[...], preferred_element_type=F32)
    ms = jnp.mean(y * y, axis=-1, keepdims=True)
    yn = y * lax.rsqrt(ms + NORM_EPS) * gp_ref[...]
    gate = mod_ref[:, 2 * d_model:]
    o_ref[...] = x_ref[...] + gate * yn


def _merge(attn, ga, hf, hb, gr, gm, x2, mod3, g_post, wa, wl, wo, rows_per_batch):
    n, d = x2.shape
    tm = MERGE_TM
    tiles_per_batch = rows_per_batch // tm
    row = pl.BlockSpec((tm, d), lambda i: (i, 0))
    wspec = pl.BlockSpec((d, d), lambda i: (0, 0), pipeline_mode=pl.Buffered(1))
    est = 3 * d * d * 2 + 2 * (5 * tm * d * 2 + tm * 2 * d * 2 + 2 * tm * d * 4) + 10 * tm * d * 4 + (4 << 20)
    return pl.pallas_call(
        functools.partial(_merge_kernel, d_model=d),
        grid=(n // tm,),
        in_specs=[
            row, row, row, row, row,
            pl.BlockSpec((tm, 2 * d), lambda i: (i, 0)),
            row,
            pl.BlockSpec((None, 1, 3 * d), lambda i: (i // tiles_per_batch, 0, 0)),
            pl.BlockSpec((1, d), lambda i: (0, 0)),
            wspec, wspec, wspec,
        ],
        out_specs=row,
        out_shape=jax.ShapeDtypeStruct((n, d), F32),
        compiler_params=pltpu.CompilerParams(
            dimension_semantics=("arbitrary",), vmem_limit_bytes=_vmem_limit(est)),
        name="merge",
    )(attn, ga, hf, hb, gr, gm, x2, mod3, g_post, wa, wl, wo)


def _rope_tables(n_tokens):
    t = np.arange(n_tokens)
    row = (t // GRID_W).astype(np.float64)
    col = (t % GRID_W).astype(np.float64)
    axis_dim = HEAD_DIM // 2
    inv_freq = ROPE_THETA ** (-np.arange(0, axis_dim, 2, dtype=np.float64) / axis_dim)
    d = np.arange(LANES) % HEAD_DIM
    freq = inv_freq[d % (axis_dim // 2)]
    pos = np.where((d < axis_dim)[None, :], row[:, None], col[:, None])
    ang = pos * freq[None, :]
    sign = np.where((d % axis_dim) < axis_dim // 2, -1.0, 1.0)
    cos = np.cos(ang)
    sin = np.sin(ang) * sign[None, :]
    qs = (HEAD_DIM ** -0.5) * math.log2(math.e)
    return tuple(jnp.asarray(a, F32) for a in (cos * qs, sin * qs, cos, sin))


def _gate_weights(w_a, w_x, b_a, b_x, wc):
    per = wc // (w_a.shape[-1])
    n_slab = LRU_BLOCKS // per
    eye = jnp.eye(per, dtype=F32)

    def slabs(w):
        w5 = w.reshape(2, n_slab, per, w.shape[-2], w.shape[-1])
        bd = jnp.einsum("dsnij,nm->dsnimj", w5, eye)
        return bd.reshape(2, n_slab, wc, wc)

    wg = jnp.concatenate([slabs(w_a), slabs(w_x)], axis=-1).astype(BF16)
    bg = jnp.concatenate([b_a.reshape(2, n_slab, 1, wc), b_x.reshape(2, n_slab, 1, wc)], axis=-1)
    return wg, bg


def kernel(x, c, ctx, c_ctx, w_mod, b_mod, g_pre, g_post, w_in, lambda_q1, lambda_k1, lambda_q2, lambda_k2,
           g_subln, w_attn_out, conv_w, conv_b, w_rg_a, b_rg_a, w_rg_x, b_rg_x, lru_lambda, w_lru_out, w_out):
    batch, n_lat, d = x.shape
    n_ctx = ctx.shape[1]
    assert w_mod.shape[0] == 1, "single-layer block"
    assert d == HEADS * V_DIM and n_lat % PROJ_TM == 0 and n_lat % MERGE_TM == 0 and n_ctx % BF16_ROWS == 0

    c_all = jnp.zeros((MOD_ROWS, d), F32).at[:batch].set(c).at[batch].set(c_ctx)
    vec = lambda a: a[0].reshape(1, HEAD_DIM)
    mod, lam_row = _modulation(c_all, w_mod[0], b_mod[0].reshape(1, -1),
                               vec(lambda_q1), vec(lambda_k1), vec(lambda_q2), vec(lambda_k2))
    mod3 = mod.reshape(MOD_ROWS, 1, 3 * d)

    w_bf = w_in[0].astype(BF16)
    tables = _rope_tables(n_lat)
    g_pre2 = g_pre[0].reshape(1, d)
    x2 = x.reshape(batch * n_lat, d)
    lat_groups = (("q", d), ("k", d), (None, d), (None, d), (None, d), (None, d), (None, 2 * d))
    lat_groups = tuple((wd, rope) for rope, wd in lat_groups)
    q, k, v, ga, xr, gr, gm = _project(
        x2, mod3, g_pre2, w_bf, tables, lat_groups, n_lat, lambda i, tpb: i // tpb)

    w_ctx = jnp.concatenate([w_bf[:, d:3 * d], w_bf[:, 4 * d:5 * d]], axis=1)
    ctx_groups = ((d, None), (d, None), (d, None))
    kc, vc, xrc = _project(
        ctx.reshape(batch * n_ctx, d), mod3, g_pre2, w_ctx, tables, ctx_groups, n_ctx,
        lambda i, tpb: batch)

    attn = _attention(q, kc, k, vc, v, lam_row, g_subln[0].reshape(1, V_DIM), batch, n_lat, n_ctx)

    wg, bg = _gate_weights(w_rg_a[0], w_rg_x[0], b_rg_a[0], b_rg_x[0], LRU_WC)
    lam = lru_lambda[0].reshape(2, 1, d)
    cb = conv_b[0].reshape(1, d)
    zeros = jnp.zeros((2, batch, d), F32)
    _, _, h_ctx = _rglru(xrc.reshape(batch, n_ctx, d), conv_w[0], cb, wg, bg, lam, zeros)
    hf, hb, _ = _rglru(xr.reshape(batch, n_lat, d), conv_w[0], cb, wg, bg, lam, h_ctx)

    out = _merge(attn, ga, hf.reshape(batch * n_lat, d), hb.reshape(batch * n_lat, d), gr, gm, x2, mod3,
                 g_post[0].reshape(1, d), w_attn_out[0].astype(BF16), w_lru_out[0].astype(BF16),
                 w_out[0].astype(BF16), n_lat)
    return out.reshape(batch, n_lat, d)
```

```python
import functools
import math

import jax
import jax.numpy as jnp
import numpy as np
from jax import lax
from jax.experimental import pallas as pl
from jax.experimental.pallas import tpu as pltpu

F32 = jnp.float32
BF16 = jnp.bfloat16

LANES = 128
SUBLANES = 8
BF16_ROWS = 16
VMEM_BYTES = 64 * 1024 * 1024

GRID_W = 64
HEADS = 8
HEAD_DIM = 64
V_DIM = 2 * HEAD_DIM
LRU_BLOCKS = 16
CONV_WIDTH = 4
LRU_C = 8.0
ROPE_THETA = 10000.0
NORM_EPS = 1e-6
LAMBDA_INIT = 0.8 - 0.6 * math.exp(-0.3 * 0)

MOD_ROWS = 24
PROJ_TM = 512
MERGE_TM = 512
ATTN_ROWS = 128
LRU_TC = 128
LRU_WC = 256
LRU_PITCH = LRU_TC + SUBLANES


def _sigmoid(x):
    return 0.5 * jnp.tanh(0.5 * x) + 0.5


def _vmem_limit(nbytes):
    return int(min(nbytes, VMEM_BYTES - 8 * 1024 * 1024))


def _mod_kernel(c_ref, w_ref, b_ref, lq1_ref, lk1_ref, lq2_ref, lk2_ref, mod_ref, lam_ref):
    c = c_ref[...]
    s = c * _sigmoid(c)
    mod_ref[...] = jnp.dot(s, w_ref[...], precision=lax.Precision.HIGHEST,
                           preferred_element_type=F32) + b_ref[...]
    s1 = jnp.sum(lq1_ref[...] * lk1_ref[...], axis=-1, keepdims=True)
    s2 = jnp.sum(lq2_ref[...] * lk2_ref[...], axis=-1, keepdims=True)
    lam = jnp.exp(s1) - jnp.exp(s2) + LAMBDA_INIT
    lam_ref[...] = jnp.broadcast_to(lam, lam_ref.shape)


def _modulation(c_all, w_mod, b_mod, lq1, lk1, lq2, lk2):
    d = c_all.shape[1]
    n_out = w_mod.shape[1]
    tn = n_out // 2
    vec = pl.BlockSpec((1, HEAD_DIM), lambda j: (0, 0))
    return pl.pallas_call(
        _mod_kernel,
        grid=(n_out // tn,),
        in_specs=[
            pl.BlockSpec((MOD_ROWS, d), lambda j: (0, 0)),
            pl.BlockSpec((d, tn), lambda j: (0, j)),
            pl.BlockSpec((1, tn), lambda j: (0, j)),
            vec, vec, vec, vec,
        ],
        out_specs=[
            pl.BlockSpec((MOD_ROWS, tn), lambda j: (0, j)),
            pl.BlockSpec((1, LANES), lambda j: (0, 0)),
        ],
        out_shape=[
            jax.ShapeDtypeStruct((MOD_ROWS, n_out), F32),
            jax.ShapeDtypeStruct((1, LANES), F32),
        ],
        name="mod",
    )(c_all, w_mod, b_mod, lq1, lk1, lq2, lk2)


def _rope_slab(xs, cos, sin, low_half):
    up = pltpu.roll(xs, LANES - 16, 1)
    dn = pltpu.roll(xs, 16, 1)
    return xs * cos + jnp.where(low_half, up, dn) * sin


def _proj_kernel(x_ref, mod_ref, g_ref, qcos_ref, qsin_ref, kcos_ref, ksin_ref, *w_and_out_refs,
                 d_model, groups):
    w_refs, out_refs = w_and_out_refs[:len(groups)], w_and_out_refs[len(groups):]
    x = x_ref[...]
    ms = jnp.mean(x * x, axis=-1, keepdims=True)
    y = x * lax.rsqrt(ms + NORM_EPS) * g_ref[...]
    shift = mod_ref[:, 0:d_model]
    scale = mod_ref[:, d_model:2 * d_model]
    h = (y * (1.0 + scale) + shift).astype(BF16)
    lane = lax.broadcasted_iota(jnp.int32, (x.shape[0], LANES), 1)
    low_half = (lane & 16) == 0
    for (width, rope, _), w_ref, o_ref in zip(groups, w_refs, out_refs):
        r = jnp.dot(h, w_ref[...], preferred_element_type=F32)
        if rope is None:
            o_ref[...] = r.astype(o_ref.dtype)
        else:
            cos_ref, sin_ref = (qcos_ref, qsin_ref) if rope == "q" else (kcos_ref, ksin_ref)
            cos = cos_ref[...]
            sin = sin_ref[...]
            for s in range(width // LANES):
                sl = slice(s * LANES, (s + 1) * LANES)
                o_ref[:, sl] = _rope_slab(r[:, sl], cos, sin, low_half).astype(o_ref.dtype)


def _project(x2, mod3, g_pre, w, tables, groups, rows_per_batch, mod_row_of_tile):
    n, d = x2.shape
    tm = min(PROJ_TM, rows_per_batch)
    tiles_per_batch = rows_per_batch // tm
    tab = pl.BlockSpec((tm, LANES), lambda i: (i % tiles_per_batch, 0))
    out_cols = sum(wd for wd, _, _ in groups)
    est = (d * out_cols * 2 + 2 * tm * d * 4 + 2 * tm * out_cols * 2 + 6 * tm * d * 4
           + 8 * tm * LANES * 4 + (4 << 20))

    def w_cols(width, offset):
        assert offset % width == 0
        return pl.BlockSpec((d, width), lambda i: (0, offset // width), pipeline_mode=pl.Buffered(1))

    return pl.pallas_call(
        functools.partial(_proj_kernel, d_model=d, groups=groups),
        grid=(n // tm,),
        in_specs=[
            pl.BlockSpec((tm, d), lambda i: (i, 0)),
            pl.BlockSpec((None, 1, 3 * d), lambda i: (mod_row_of_tile(i, tiles_per_batch), 0, 0)),
            pl.BlockSpec((1, d), lambda i: (0, 0)),
            tab, tab, tab, tab,
        ] + [w_cols(wd, off) for wd, _, off in groups],
        out_specs=[pl.BlockSpec((tm, wd), lambda i: (i, 0)) for wd, _, _ in groups],
        out_shape=[jax.ShapeDtypeStruct((n, wd), BF16) for wd, _, _ in groups],
        compiler_params=pltpu.CompilerParams(
            dimension_semantics=("arbitrary",), vmem_limit_bytes=_vmem_limit(est)),
        name="proj",
    )(x2, mod3, g_pre, *tables, *([w] * len(groups)))


def _attn_kernel(q_ref, kc_ref, k_ref, vc_ref, v_ref, lam_ref, g_ref, o_ref, kall_ref, vext_ref,
                 s0_ref, s1_ref, *, n_ctx, rows):
    n_sub = q_ref.shape[0] // rows
    kall_ref[0:n_ctx, :] = kc_ref[...]
    kall_ref[n_ctx:, :] = k_ref[...]
    vext_ref[0:n_ctx, 0:V_DIM] = vc_ref[...]
    vext_ref[n_ctx:, 0:V_DIM] = v_ref[...]
    vext_ref[:, V_DIM:] = jnp.ones((vext_ref.shape[0], V_DIM), BF16)
    lane = lax.broadcasted_iota(jnp.int32, (rows, V_DIM), 1)
    first_half = lane < HEAD_DIM

    def scores(r, s_ref):
        q = q_ref[pl.ds(pl.multiple_of(r * rows, rows), rows), :]
        zero = jnp.zeros_like(q)
        q2 = jnp.concatenate([jnp.where(first_half, q, zero), jnp.where(first_half, zero, q)], axis=0)
        s_ref[...] = lax.dot_general(q2, kall_ref[...], (((1,), (1,)), ((), ())),
                                     preferred_element_type=F32)

    def attend(r, s_ref):
        s = s_ref[...]
        m = jnp.max(s, axis=-1, keepdims=True)
        e = jnp.exp2(s - m).astype(BF16)
        acc = jnp.dot(e, vext_ref[...], preferred_element_type=F32)
        oj = acc[:, 0:V_DIM] / acc[:, V_DIM:]
        o = oj[0:rows] - lam_ref[...] * oj[rows:]
        ms = jnp.mean(o * o, axis=-1, keepdims=True)
        o = o * lax.rsqrt(ms + NORM_EPS) * g_ref[...] * (1.0 - LAMBDA_INIT)
        o_ref[pl.ds(pl.multiple_of(r * rows, rows), rows), :] = o.astype(o_ref.dtype)

    scores(0, s0_ref)

    def pair(p, carry):
        r = 2 * p
        scores(r + 1, s1_ref)
        attend(r, s0_ref)
        scores(r + 2, s0_ref)
        attend(r + 1, s1_ref)
        return carry

    lax.fori_loop(0, n_sub // 2 - 1, pair, 0, unroll=True)
    scores(n_sub - 1, s1_ref)
    attend(n_sub - 2, s0_ref)
    attend(n_sub - 1, s1_ref)


def _attention(q, kc, k, vc, v, lam_row, g_subln, batch, n_lat, n_ctx):
    rows = ATTN_ROWS
    assert n_lat % (2 * rows) == 0
    n_keys = n_ctx + n_lat
    est = (2 * (n_keys * V_DIM * 2 * 2) + n_keys * V_DIM * 2 + n_keys * 2 * V_DIM * 2
           + 4 * n_lat * V_DIM * 2 + 2 * (2 * rows * n_keys * 4) + 4 * (2 * rows * n_keys * 4) + (8 << 20))
    return pl.pallas_call(
        functools.partial(_attn_kernel, n_ctx=n_ctx, rows=rows),
        grid=(batch, HEADS),
        in_specs=[
            pl.BlockSpec((n_lat, V_DIM), lambda b, h: (b, h)),
            pl.BlockSpec((n_ctx, V_DIM), lambda b, h: (b, h)),
            pl.BlockSpec((n_lat, V_DIM), lambda b, h: (b, h)),
            pl.BlockSpec((n_ctx, V_DIM), lambda b, h: (b, h)),
            pl.BlockSpec((n_lat, V_DIM), lambda b, h: (b, h)),
            pl.BlockSpec((1, LANES), lambda b, h: (0, 0)),
            pl.BlockSpec((1, V_DIM), lambda b, h: (0, 0)),
        ],
        out_specs=pl.BlockSpec((n_lat, V_DIM), lambda b, h: (b, h)),
        out_shape=jax.ShapeDtypeStruct((batch * n_lat, HEADS * V_DIM), BF16),
        scratch_shapes=[
            pltpu.VMEM((n_keys, V_DIM), BF16),
            pltpu.VMEM((n_keys, 2 * V_DIM), BF16),
            pltpu.VMEM((2 * rows, n_keys), F32),
            pltpu.VMEM((2 * rows, n_keys), F32),
        ],
        compiler_params=pltpu.CompilerParams(
            dimension_semantics=("arbitrary", "arbitrary"),
            vmem_limit_bytes=_vmem_limit(est)),
        name="attn",
    )(q, kc, k, vc, v, lam_row, g_subln)


def _lru_kernel(xf_ref, xfp_ref, xfn_ref, xb_ref, xbp_ref, xbn_ref, cw_ref, cb_ref, wg_ref, bg_ref,
                lam_ref, h0_ref, hf_ref, hb_ref, hfin_ref, xt_ref, xc_ref, a_ref, u_ref, ht_ref, st_ref,
                *, batch, tc, wc):
    i = pl.program_id(1)
    n = pl.num_programs(1)
    n_slab = wc // LANES
    n_grp = batch // SUBLANES
    pitch = LRU_PITCH
    gate_rows = 512
    cols = [slice(l * LANES, (l + 1) * LANES) for l in range(n_slab)]

    @pl.when(i == 0)
    def _():
        st_ref[...] = h0_ref[...]

    def stage_input(d, cur_ref, prev_ref, next_ref, chunk):
        sub = lax.broadcasted_iota(jnp.int32, (SUBLANES, LANES), 0)
        for l in range(n_slab):
            for b in range(batch):
                xt_ref[d, l, b * pitch + SUBLANES:b * pitch + SUBLANES + tc, :] = (
                    cur_ref[b, :, cols[l]].astype(F32))
            for kb in range(batch + 1):
                if kb < batch:
                    prev = prev_ref[kb, :, cols[l]].astype(F32)[BF16_ROWS - SUBLANES:, :]
                    prev = jnp.where(chunk == 0, 0.0, prev)
                else:
                    prev = jnp.zeros((SUBLANES, LANES), F32)
                if kb > 0:
                    nxt = next_ref[kb - 1, :, cols[l]].astype(F32)[0:SUBLANES, :]
                    nxt = jnp.where(chunk == n - 1, 0.0, nxt)
                else:
                    nxt = jnp.zeros((SUBLANES, LANES), F32)
                xt_ref[d, l, kb * pitch:kb * pitch + SUBLANES, :] = jnp.where(sub < 2, nxt, prev)

    stage_input(0, xf_ref, xfp_ref, xfn_ref, i)
    stage_input(1, xb_ref, xbp_ref, xbn_ref, n - 1 - i)

    def x_rows(d, l, g, tau):
        start = g * SUBLANES * pitch + SUBLANES + tau
        return xt_ref[d, l, pl.ds(start, SUBLANES, stride=pitch), :]

    taps = [[jnp.broadcast_to(cw_ref[j:j + 1, cols[l]], (SUBLANES, LANES)) for j in range(CONV_WIDTH)]
            for l in range(n_slab)]
    bias = [jnp.broadcast_to(cb_ref[:, cols[l]], (SUBLANES, LANES)) for l in range(n_slab)]
    units = [(l, g) for l in range(n_slab) for g in range(n_grp)]

    def conv_window(d):
        return [tuple(x_rows(d, l, g, tau) for tau in (-1, 0, 1)) for l, g in units]

    def conv_at(t, d, win):
        new = []
        out = {}
        for (l, g), (xm1, x0, xp1) in zip(units, win):
            xp2 = x_rows(d, l, g, t + 2)
            w = taps[l]
            out[l, g] = bias[l] + xm1 * w[0] + x0 * w[1] + xp1 * w[2] + xp2 * w[3]
            new.append((x0, xp1, xp2))
        row = pl.ds(pl.multiple_of(t * batch, batch), batch)
        xc_ref[d, row, :] = jnp.concatenate(
            [jnp.concatenate([out[l, g] for l in range(n_slab)], axis=1) for g in range(n_grp)], axis=0)
        return new

    def gates(d):
        half_cl = (-0.5 * LRU_C * math.log2(math.e)) * jax.nn.softplus(-lam_ref[d])
        for r0 in range(0, tc * batch, gate_rows):
            rows = slice(r0, r0 + gate_rows)
            xc = xc_ref[d, rows, :]
            z = jnp.dot(xc.astype(BF16), wg_ref[d], preferred_element_type=F32) + bg_ref[d]
            a = jnp.exp2(half_cl * jnp.tanh(z[:, 0:wc]) + half_cl)
            a_ref[d, rows, :] = a
            y = 1.0 - a * a
            mult = y * lax.rsqrt(jnp.maximum(y, 1e-30))
            u_ref[d, rows, :] = (mult * xc) * (0.5 * jnp.tanh(z[:, wc:]) + 0.5)

    def scan_at(time, d, h):
        row = pl.ds(pl.multiple_of(time * batch, batch), batch)
        h = a_ref[d, row, :] * h + u_ref[d, row, :]
        for l in range(n_slab):
            for g in range(n_grp):
                dst = pl.ds(g * SUBLANES * pitch + time, SUBLANES, stride=pitch)
                ht_ref[d, l, dst, :] = h[g * SUBLANES:(g + 1) * SUBLANES, cols[l]]
        return h

    lax.fori_loop(0, tc, lambda t, win: conv_at(t, 0, win), conv_window(0), unroll=4)
    gates(0)
    h_fwd, _ = lax.fori_loop(
        0, tc, lambda t, c: (scan_at(t, 0, c[0]), conv_at(t, 1, c[1])), (st_ref[0], conv_window(1)), unroll=4)
    st_ref[0] = h_fwd
    gates(1)
    st_ref[1] = lax.fori_loop(0, tc, lambda t, h: scan_at(tc - 1 - t, 1, h), st_ref[1], unroll=8)

    for d, o_ref in ((0, hf_ref), (1, hb_ref)):
        for b in range(batch):
            for l in range(n_slab):
                o_ref[b, :, cols[l]] = ht_ref[d, l, b * pitch:b * pitch + tc, :].astype(o_ref.dtype)

    @pl.when(i == n - 1)
    def _():
        hfin_ref[...] = st_ref[...]


def _rglru(xr3, conv_w, conv_b, wg, bg, lam, h0):
    batch, length, width = xr3.shape
    tc, wc = LRU_TC, LRU_WC
    n = length // tc
    hb_per = tc // BF16_ROWS
    n_halo = length // BF16_ROWS
    n_slab = wc // LANES

    def cur(fwd):
        return pl.BlockSpec((batch, tc, wc), lambda s, i: (0, i if fwd else n - 1 - i, s))

    def prev(fwd):
        return pl.BlockSpec((batch, BF16_ROWS, wc), lambda s, i: (
            0, jnp.maximum((i if fwd else n - 1 - i) * hb_per - 1, 0), s))

    def nxt(fwd):
        return pl.BlockSpec((batch, BF16_ROWS, wc), lambda s, i: (
            0, jnp.minimum(((i if fwd else n - 1 - i) + 1) * hb_per, n_halo - 1), s))

    pitched_rows = batch * LRU_PITCH + SUBLANES
    time_major = pltpu.VMEM((2, tc * batch, wc), F32)
    est = (2 * 2 * n_slab * pitched_rows * LANES * 4 + 3 * 2 * tc * batch * wc * 4
           + 8 * batch * tc * wc * 2 + 8 * 512 * 2 * wc * 4 + (8 << 20))
    return pl.pallas_call(
        functools.partial(_lru_kernel, batch=batch, tc=tc, wc=wc),
        grid=(width // wc, n),
        in_specs=[
            cur(True), prev(True), nxt(True), cur(False), prev(False), nxt(False),
            pl.BlockSpec((CONV_WIDTH, wc), lambda s, i: (0, s)),
            pl.BlockSpec((1, wc), lambda s, i: (0, s)),
            pl.BlockSpec((2, None, wc, 2 * wc), lambda s, i: (0, s, 0, 0)),
            pl.BlockSpec((2, None, 1, 2 * wc), lambda s, i: (0, s, 0, 0)),
            pl.BlockSpec((2, 1, wc), lambda s, i: (0, 0, s)),
            pl.BlockSpec((2, batch, wc), lambda s, i: (0, 0, s)),
        ],
        out_specs=[
            pl.BlockSpec((batch, tc, wc), lambda s, i: (0, i, s)),
            pl.BlockSpec((batch, tc, wc), lambda s, i: (0, n - 1 - i, s)),
            pl.BlockSpec((2, batch, wc), lambda s, i: (0, 0, s)),
        ],
        out_shape=[
            jax.ShapeDtypeStruct((batch, length, width), BF16),
            jax.ShapeDtypeStruct((batch, length, width), BF16),
            jax.ShapeDtypeStruct((2, batch, width), F32),
        ],
        scratch_shapes=[
            pltpu.VMEM((2, n_slab, pitched_rows, LANES), F32),
            time_major, time_major, time_major,
            pltpu.VMEM((2, n_slab, pitched_rows, LANES), F32),
            pltpu.VMEM((2, batch, wc), F32),
        ],
        compiler_params=pltpu.CompilerParams(
            dimension_semantics=("arbitrary", "arbitrary"), vmem_limit_bytes=_vmem_limit(est)),
        name="lru",
    )(xr3, xr3, xr3, xr3, xr3, xr3, conv_w, conv_b, wg, bg, lam, h0)


def _merge_kernel(attn_ref, ga_ref, hf_ref, hb_ref, gr_ref, gm_ref, x_ref, mod_ref, gp_ref,
                  wa_ref, wl_ref, wo_ref, o_ref, *, d_model):
    def silu(ref):
        h = ref[...] * 0.5
        return h + h * jnp.tanh(h)

    a_in = attn_ref[...] * silu(ga_ref)
    y_attn = jnp.dot(a_in, wa_ref[...], preferred_element_type=F32)
    l_in = (hf_ref[...] + hb_ref[...]) * silu(gr_ref)
    y_lru = jnp.dot(l_in, wl_ref[...], preferred_element_type=F32)
    t_attn = jnp.tanh(0.5 * gm_ref[:, 0:d_model].astype(F32))
    t_lru = jnp.tanh(0.5 * gm_ref[:, d_model:].astype(F32))
    z = (0.5 * ((t_attn + 1.0) * y_attn + (t_lru + 1.0) * y_lru)).astype(BF16)
    y = jnp.dot(z, wo_ref[...], preferred_element_type=F32)
    ms = jnp.mean(y * y, axis=-1, keepdims=True)
    yn = y * lax.rsqrt(ms + NORM_EPS) * gp_ref[...]
    gate = mod_ref[:, 2 * d_model:]
    o_ref[...] = x_ref[...] + gate * yn


def _merge(attn, ga, hf, hb, gr, gm, x2, mod3, g_post, wa, wl, wo, rows_per_batch):
    n, d = x2.shape
    tm = MERGE_TM
    tiles_per_batch = rows_per_batch // tm
    row = pl.BlockSpec((tm, d), lambda i: (i, 0))
    wspec = pl.BlockSpec((d, d), lambda i: (0, 0), pipeline_mode=pl.Buffered(1))
    est = 3 * d * d * 2 + 2 * (5 * tm * d * 2 + tm * 2 * d * 2 + 2 * tm * d * 4) + 10 * tm * d * 4 + (4 << 20)
    return pl.pallas_call(
        functools.partial(_merge_kernel, d_model=d),
        grid=(n // tm,),
        in_specs=[
            row, row, row, row, row,
            pl.BlockSpec((tm, 2 * d), lambda i: (i, 0)),
            row,
            pl.BlockSpec((None, 1, 3 * d), lambda i: (i // tiles_per_batch, 0, 0)),
            pl.BlockSpec((1, d), lambda i: (0, 0)),
            wspec, wspec, wspec,
        ],
        out_specs=row,
        out_shape=jax.ShapeDtypeStruct((n, d), F32),
        compiler_params=pltpu.CompilerParams(
            dimension_semantics=("arbitrary",), vmem_limit_bytes=_vmem_limit(est)),
        name="merge",
    )(attn, ga, hf, hb, gr, gm, x2, mod3, g_post, wa, wl, wo)


def _rope_tables(n_tokens):
    t = np.arange(n_tokens)
    row = (t // GRID_W).astype(np.float64)
    col = (t % GRID_W).astype(np.float64)
    axis_dim = HEAD_DIM // 2
    inv_freq = ROPE_THETA ** (-np.arange(0, axis_dim, 2, dtype=np.float64) / axis_dim)
    d = np.arange(LANES) % HEAD_DIM
    freq = inv_freq[d % (axis_dim // 2)]
    pos = np.where((d < axis_dim)[None, :], row[:, None], col[:, None])
    ang = pos * freq[None, :]
    sign = np.where((d % axis_dim) < axis_dim // 2, -1.0, 1.0)
    cos = np.cos(ang)
    sin = np.sin(ang) * sign[None, :]
    qs = (HEAD_DIM ** -0.5) * math.log2(math.e)
    return tuple(jnp.asarray(a, F32) for a in (cos * qs, sin * qs, cos, sin))


def _gate_weights(w_a, w_x, b_a, b_x, wc):
    per = wc // (w_a.shape[-1])
    n_slab = LRU_BLOCKS // per
    eye = jnp.eye(per, dtype=F32)

    def slabs(w):
        w5 = w.reshape(2, n_slab, per, w.shape[-2], w.shape[-1])
        bd = jnp.einsum("dsnij,nm->dsnimj", w5, eye)
        return bd.reshape(2, n_slab, wc, wc)

    wg = (0.5 * jnp.concatenate([slabs(w_a), slabs(w_x)], axis=-1)).astype(BF16)
    bg = 0.5 * jnp.concatenate([b_a.reshape(2, n_slab, 1, wc), b_x.reshape(2, n_slab, 1, wc)], axis=-1)
    return wg, bg


def kernel(x, c, ctx, c_ctx, w_mod, b_mod, g_pre, g_post, w_in, lambda_q1, lambda_k1, lambda_q2, lambda_k2,
           g_subln, w_attn_out, conv_w, conv_b, w_rg_a, b_rg_a, w_rg_x, b_rg_x, lru_lambda, w_lru_out, w_out):
    batch, n_lat, d = x.shape
    n_ctx = ctx.shape[1]
    assert w_mod.shape[0] == 1, "single-layer block"
    assert d == HEADS * V_DIM and n_lat % PROJ_TM == 0 and n_lat % MERGE_TM == 0 and n_ctx % BF16_ROWS == 0

    c_all = jnp.zeros((MOD_ROWS, d), F32).at[:batch].set(c).at[batch].set(c_ctx)
    vec = lambda a: a[0].reshape(1, HEAD_DIM)
    mod, lam_row = _modulation(c_all, w_mod[0], b_mod[0].reshape(1, -1),
                               vec(lambda_q1), vec(lambda_k1), vec(lambda_q2), vec(lambda_k2))
    mod3 = mod.reshape(MOD_ROWS, 1, 3 * d)

    w_bf = w_in[0].astype(BF16)
    tables = _rope_tables(n_lat)
    g_pre2 = g_pre[0].reshape(1, d)
    x2 = x.reshape(batch * n_lat, d)
    lat_groups = ((d, "q", 0), (d, "k", d), (d, None, 2 * d), (d, None, 3 * d), (d, None, 4 * d),
                  (d, None, 5 * d), (2 * d, None, 6 * d))
    q, k, v, ga, xr, gr, gm = _project(
        x2, mod3, g_pre2, w_bf, tables, lat_groups, n_lat, lambda i, tpb: i // tpb)

    ctx_groups = ((d, None, d), (d, None, 2 * d), (d, None, 4 * d))
    kc, vc, xrc = _project(
        ctx.reshape(batch * n_ctx, d), mod3, g_pre2, w_bf, tables, ctx_groups, n_ctx,
        lambda i, tpb: batch)

    attn = _attention(q, kc, k, vc, v, lam_row, g_subln[0].reshape(1, V_DIM), batch, n_lat, n_ctx)

    wg, bg = _gate_weights(w_rg_a[0], w_rg_x[0], b_rg_a[0], b_rg_x[0], LRU_WC)
    lam = lru_lambda[0].reshape(2, 1, d)
    cb = conv_b[0].reshape(1, d)
    zeros = jnp.zeros((2, batch, d), F32)
    _, _, h_ctx = _rglru(xrc.reshape(batch, n_ctx, d), conv_w[0], cb, wg, bg, lam, zeros)
    hf, hb, _ = _rglru(xr.reshape(batch, n_lat, d), conv_w[0], cb, wg, bg, lam, h_ctx)

    out = _merge(attn, ga, hf.reshape(batch * n_lat, d), hb.reshape(batch * n_lat, d), gr, gm, x2, mod3,
                 g_post[0].reshape(1, d), w_attn_out[0].astype(BF16), w_lru_out[0].astype(BF16),
                 w_out[0].astype(BF16), n_lat)
    return out.reshape(batch, n_lat, d)
```

```python
import functools
import math

import jax
import jax.numpy as jnp
import numpy as np
from jax import lax
from jax.experimental import pallas as pl
from jax.experimental.pallas import tpu as pltpu

F32 = jnp.float32
BF16 = jnp.bfloat16

LANES = 128
SUBLANES = 8
BF16_ROWS = 16
VMEM_BYTES = 64 * 1024 * 1024

GRID_W = 64
HEADS = 8
HEAD_DIM = 64
V_DIM = 2 * HEAD_DIM
LRU_BLOCKS = 16
CONV_WIDTH = 4
LRU_C = 8.0
ROPE_THETA = 10000.0
NORM_EPS = 1e-6
LAMBDA_INIT = 0.8 - 0.6 * math.exp(-0.3 * 0)

MOD_ROWS = 24
PROJ_TM = 512
MERGE_TM = 512
ATTN_ROWS = 128
LRU_TC = 128
LRU_WC = 256
LRU_GATE_ROWS = 512


def _sigmoid(x):
    return 0.5 * jnp.tanh(0.5 * x) + 0.5


def _vmem_limit(nbytes):
    return int(min(nbytes, VMEM_BYTES - 8 * 1024 * 1024))


def _mod_kernel(c_ref, w_ref, b_ref, lq1_ref, lk1_ref, lq2_ref, lk2_ref, mod_ref, lam_ref):
    c = c_ref[...]
    s = c * _sigmoid(c)
    mod_ref[...] = jnp.dot(s, w_ref[...], precision=lax.Precision.HIGHEST,
                           preferred_element_type=F32) + b_ref[...]
    s1 = jnp.sum(lq1_ref[...] * lk1_ref[...], axis=-1, keepdims=True)
    s2 = jnp.sum(lq2_ref[...] * lk2_ref[...], axis=-1, keepdims=True)
    lam = jnp.exp(s1) - jnp.exp(s2) + LAMBDA_INIT
    lam_ref[...] = jnp.broadcast_to(lam, lam_ref.shape)


def _modulation(c_all, w_mod, b_mod, lq1, lk1, lq2, lk2):
    d = c_all.shape[1]
    n_out = w_mod.shape[1]
    tn = n_out // 2
    vec = pl.BlockSpec((1, HEAD_DIM), lambda j: (0, 0))
    return pl.pallas_call(
        _mod_kernel,
        grid=(n_out // tn,),
        in_specs=[
            pl.BlockSpec((MOD_ROWS, d), lambda j: (0, 0)),
            pl.BlockSpec((d, tn), lambda j: (0, j)),
            pl.BlockSpec((1, tn), lambda j: (0, j)),
            vec, vec, vec, vec,
        ],
        out_specs=[
            pl.BlockSpec((MOD_ROWS, tn), lambda j: (0, j)),
            pl.BlockSpec((1, LANES), lambda j: (0, 0)),
        ],
        out_shape=[
            jax.ShapeDtypeStruct((MOD_ROWS, n_out), F32),
            jax.ShapeDtypeStruct((1, LANES), F32),
        ],
        name="mod",
    )(c_all, w_mod, b_mod, lq1, lk1, lq2, lk2)


def _rope_slab(xs, cos, sin, low_half):
    up = pltpu.roll(xs, LANES - 16, 1)
    dn = pltpu.roll(xs, 16, 1)
    return xs * cos + jnp.where(low_half, up, dn) * sin


def _proj_kernel(x_ref, mod_ref, g_ref, qcos_ref, qsin_ref, kcos_ref, ksin_ref, *w_and_out_refs,
                 d_model, groups):
    w_refs, out_refs = w_and_out_refs[:len(groups)], w_and_out_refs[len(groups):]
    x = x_ref[...]
    ms = jnp.mean(x * x, axis=-1, keepdims=True)
    y = x * lax.rsqrt(ms + NORM_EPS) * g_ref[...]
    shift = mod_ref[:, 0:d_model]
    scale = mod_ref[:, d_model:2 * d_model]
    h = (y * (1.0 + scale) + shift).astype(BF16)
    lane = lax.broadcasted_iota(jnp.int32, (x.shape[0], LANES), 1)
    low_half = (lane & 16) == 0
    for (width, rope, _, _), w_ref, o_ref in zip(groups, w_refs, out_refs):
        r = jnp.dot(h, w_ref[...], preferred_element_type=F32)
        if rope is None:
            o_ref[...] = r.astype(o_ref.dtype)
        else:
            cos_ref, sin_ref = (qcos_ref, qsin_ref) if rope == "q" else (kcos_ref, ksin_ref)
            cos = cos_ref[...]
            sin = sin_ref[...]
            for s in range(width // LANES):
                sl = slice(s * LANES, (s + 1) * LANES)
                o_ref[:, sl] = _rope_slab(r[:, sl], cos, sin, low_half).astype(o_ref.dtype)


def _project(x2, mod3, g_pre, w, tables, groups, rows_per_batch, mod_row_of_tile):
    n, d = x2.shape
    tm = min(PROJ_TM, rows_per_batch)
    tiles_per_batch = rows_per_batch // tm
    n_batches = n // rows_per_batch
    tab = pl.BlockSpec((tm, LANES), lambda i: (i % tiles_per_batch, 0))
    out_cols = sum(g[0] for g in groups)

    def out_spec(width, time_major):
        if time_major:
            return pl.BlockSpec((tm, width), lambda i: (i % tiles_per_batch, i // tiles_per_batch))
        return pl.BlockSpec((tm, width), lambda i: (i, 0))

    def out_shape(width, time_major):
        shape = (rows_per_batch, n_batches * width) if time_major else (n, width)
        return jax.ShapeDtypeStruct(shape, BF16)

    est = (d * out_cols * 2 + 2 * tm * d * 4 + 2 * tm * out_cols * 2 + 6 * tm * d * 4
           + 8 * tm * LANES * 4 + (4 << 20))

    def w_cols(width, offset):
        assert offset % width == 0
        return pl.BlockSpec((d, width), lambda i: (0, offset // width), pipeline_mode=pl.Buffered(1))

    return pl.pallas_call(
        functools.partial(_proj_kernel, d_model=d, groups=groups),
        grid=(n // tm,),
        in_specs=[
            pl.BlockSpec((tm, d), lambda i: (i, 0)),
            pl.BlockSpec((None, 1, 3 * d), lambda i: (mod_row_of_tile(i, tiles_per_batch), 0, 0)),
            pl.BlockSpec((1, d), lambda i: (0, 0)),
            tab, tab, tab, tab,
        ] + [w_cols(wd, off) for wd, _, off, _ in groups],
        out_specs=[out_spec(wd, tmaj) for wd, _, _, tmaj in groups],
        out_shape=[out_shape(wd, tmaj) for wd, _, _, tmaj in groups],
        compiler_params=pltpu.CompilerParams(
            dimension_semantics=("arbitrary",), vmem_limit_bytes=_vmem_limit(est)),
        name="proj",
    )(x2, mod3, g_pre, *tables, *([w] * len(groups)))


def _attn_kernel(q_ref, kc_ref, k_ref, vc_ref, v_ref, lam_ref, g_ref, o_ref, kall_ref, vext_ref,
                 s0_ref, s1_ref, *, n_ctx, rows):
    n_sub = q_ref.shape[0] // rows
    kall_ref[0:n_ctx, :] = kc_ref[...]
    kall_ref[n_ctx:, :] = k_ref[...]
    vext_ref[0:n_ctx, 0:V_DIM] = vc_ref[...]
    vext_ref[n_ctx:, 0:V_DIM] = v_ref[...]
    vext_ref[:, V_DIM:] = jnp.ones((vext_ref.shape[0], V_DIM), BF16)
    lane = lax.broadcasted_iota(jnp.int32, (rows, V_DIM), 1)
    first_half = lane < HEAD_DIM

    def scores(r, s_ref):
        q = q_ref[pl.ds(pl.multiple_of(r * rows, rows), rows), :]
        zero = jnp.zeros_like(q)
        q2 = jnp.concatenate([jnp.where(first_half, q, zero), jnp.where(first_half, zero, q)], axis=0)
        s_ref[...] = lax.dot_general(q2, kall_ref[...], (((1,), (1,)), ((), ())),
                                     preferred_element_type=F32)

    def attend(r, s_ref):
        s = s_ref[...]
        m = jnp.max(s, axis=-1, keepdims=True)
        e = jnp.exp2(s - m).astype(BF16)
        acc = jnp.dot(e, vext_ref[...], preferred_element_type=F32)
        oj = acc[:, 0:V_DIM] / acc[:, V_DIM:]
        o = oj[0:rows] - lam_ref[...] * oj[rows:]
        ms = jnp.mean(o * o, axis=-1, keepdims=True)
        o = o * lax.rsqrt(ms + NORM_EPS) * g_ref[...] * (1.0 - LAMBDA_INIT)
        o_ref[pl.ds(pl.multiple_of(r * rows, rows), rows), :] = o.astype(o_ref.dtype)

    scores(0, s0_ref)

    def pair(p, carry):
        r = 2 * p
        scores(r + 1, s1_ref)
        attend(r, s0_ref)
        scores(r + 2, s0_ref)
        attend(r + 1, s1_ref)
        return carry

    lax.fori_loop(0, n_sub // 2 - 1, pair, 0, unroll=True)
    scores(n_sub - 1, s1_ref)
    attend(n_sub - 2, s0_ref)
    attend(n_sub - 1, s1_ref)


def _attention(q, kc, k, vc, v, lam_row, g_subln, batch, n_lat, n_ctx):
    rows = ATTN_ROWS
    assert n_lat % (2 * rows) == 0
    n_keys = n_ctx + n_lat
    est = (2 * (n_keys * V_DIM * 2 * 2) + n_keys * V_DIM * 2 + n_keys * 2 * V_DIM * 2
           + 4 * n_lat * V_DIM * 2 + 2 * (2 * rows * n_keys * 4) + 4 * (2 * rows * n_keys * 4) + (8 << 20))
    return pl.pallas_call(
        functools.partial(_attn_kernel, n_ctx=n_ctx, rows=rows),
        grid=(batch, HEADS),
        in_specs=[
            pl.BlockSpec((n_lat, V_DIM), lambda b, h: (b, h)),
            pl.BlockSpec((n_ctx, V_DIM), lambda b, h: (b, h)),
            pl.BlockSpec((n_lat, V_DIM), lambda b, h: (b, h)),
            pl.BlockSpec((n_ctx, V_DIM), lambda b, h: (b, h)),
            pl.BlockSpec((n_lat, V_DIM), lambda b, h: (b, h)),
            pl.BlockSpec((1, LANES), lambda b, h: (0, 0)),
            pl.BlockSpec((1, V_DIM), lambda b, h: (0, 0)),
        ],
        out_specs=pl.BlockSpec((n_lat, V_DIM), lambda b, h: (b, h)),
        out_shape=jax.ShapeDtypeStruct((batch * n_lat, HEADS * V_DIM), BF16),
        scratch_shapes=[
            pltpu.VMEM((n_keys, V_DIM), BF16),
            pltpu.VMEM((n_keys, 2 * V_DIM), BF16),
            pltpu.VMEM((2 * rows, n_keys), F32),
            pltpu.VMEM((2 * rows, n_keys), F32),
        ],
        compiler_params=pltpu.CompilerParams(
            dimension_semantics=("arbitrary", "arbitrary"),
            vmem_limit_bytes=_vmem_limit(est)),
        name="attn",
    )(q, kc, k, vc, v, lam_row, g_subln)


def _scan_kernel(xf_ref, xfp_ref, xfn_ref, xb_ref, xbp_ref, xbn_ref, cw_ref, cb_ref, wg_ref, bg_ref,
                 lam_ref, h0_ref, hf_ref, hb_ref, hfin_ref, xe_ref, a_ref, u_ref, st_ref, *, batch, tc, wc):
    i = pl.program_id(1)
    n = pl.num_programs(1)
    t_grp = LRU_GATE_ROWS // batch

    @pl.when(i == 0)
    def _():
        st_ref[...] = h0_ref[...]

    def coeffs(d, cur_ref, prev_ref, next_ref, chunk):
        xe_ref[d, 1:tc + 1] = cur_ref[...].astype(F32)
        xe_ref[d, 0:1] = jnp.where(chunk == 0, 0.0, prev_ref[...].astype(F32))
        xe_ref[d, tc + 1:tc + 3] = jnp.where(chunk == n - 1, 0.0, next_ref[...].astype(F32))
        half_cl = (-0.5 * LRU_C * math.log2(math.e)) * jax.nn.softplus(-lam_ref[d])
        for t0 in range(0, tc, t_grp):
            xc = cb_ref[...]
            for j in range(CONV_WIDTH):
                xc = xc + xe_ref[d, t0 + j:t0 + j + t_grp] * cw_ref[j:j + 1, :]
            xc = xc.reshape(t_grp * batch, wc)
            z = jnp.dot(xc.astype(BF16), wg_ref[d], preferred_element_type=F32) + bg_ref[d]
            a = jnp.exp2(half_cl * jnp.tanh(z[:, 0:wc]) + half_cl)
            y = 1.0 - a * a
            mult = y * lax.rsqrt(jnp.maximum(y, 1e-30))
            u = (mult * xc) * (0.5 * jnp.tanh(z[:, wc:]) + 0.5)
            a_ref[d, t0:t0 + t_grp] = a.reshape(t_grp, batch, wc)
            u_ref[d, t0:t0 + t_grp] = u.reshape(t_grp, batch, wc)

    coeffs(0, xf_ref, xfp_ref, xfn_ref, i)
    coeffs(1, xb_ref, xbp_ref, xbn_ref, n - 1 - i)

    def scan_step(t, hs):
        tb = tc - 1 - t
        h_f = a_ref[0, t] * hs[0] + u_ref[0, t]
        hf_ref[t] = h_f.astype(hf_ref.dtype)
        h_b = a_ref[1, tb] * hs[1] + u_ref[1, tb]
        hb_ref[tb] = h_b.astype(hb_ref.dtype)
        return h_f, h_b

    h_f, h_b = lax.fori_loop(0, tc, scan_step, (st_ref[0], st_ref[1]), unroll=8)
    st_ref[0] = h_f
    st_ref[1] = h_b

    @pl.when(i == n - 1)
    def _():
        hfin_ref[...] = st_ref[...]


def _scan(x_t, conv_w, conv_b, wg, bg, lam, h0):
    length, batch, width = x_t.shape
    tc, wc = LRU_TC, LRU_WC
    n = length // tc
    assert tc % 2 == 0 and LRU_GATE_ROWS % batch == 0 and tc % (LRU_GATE_ROWS // batch) == 0

    def chunk(fwd, i):
        return i if fwd else n - 1 - i

    def cur(fwd):
        return pl.BlockSpec((tc, batch, wc), lambda s, i: (chunk(fwd, i), 0, s))

    def prev(fwd):
        return pl.BlockSpec((1, batch, wc), lambda s, i: (jnp.maximum(chunk(fwd, i) * tc - 1, 0), 0, s))

    def nxt(fwd):
        return pl.BlockSpec((2, batch, wc), lambda s, i: (
            jnp.minimum((chunk(fwd, i) + 1) * (tc // 2), length // 2 - 1), 0, s))

    chunk_f32 = 2 * tc * batch * wc * 4
    est = (2 * (tc + 3) * batch * wc * 4 + 2 * chunk_f32 + 8 * tc * batch * wc * 2
           + 8 * LRU_GATE_ROWS * 2 * wc * 4 + (8 << 20))
    return pl.pallas_call(
        functools.partial(_scan_kernel, batch=batch, tc=tc, wc=wc),
        grid=(width // wc, n),
        in_specs=[
            cur(True), prev(True), nxt(True), cur(False), prev(False), nxt(False),
            pl.BlockSpec((CONV_WIDTH, wc), lambda s, i: (0, s)),
            pl.BlockSpec((1, wc), lambda s, i: (0, s)),
            pl.BlockSpec((2, None, wc, 2 * wc), lambda s, i: (0, s, 0, 0)),
            pl.BlockSpec((2, None, 1, 2 * wc), lambda s, i: (0, s, 0, 0)),
            pl.BlockSpec((2, 1, wc), lambda s, i: (0, 0, s)),
            pl.BlockSpec((2, batch, wc), lambda s, i: (0, 0, s)),
        ],
        out_specs=[cur(True), cur(False), pl.BlockSpec((2, batch, wc), lambda s, i: (0, 0, s))],
        out_shape=[
            jax.ShapeDtypeStruct((length, batch, width), BF16),
            jax.ShapeDtypeStruct((length, batch, width), BF16),
            jax.ShapeDtypeStruct((2, batch, width), F32),
        ],
        scratch_shapes=[
            pltpu.VMEM((2, tc + 3, batch, wc), F32),
            pltpu.VMEM((2, tc, batch, wc), F32),
            pltpu.VMEM((2, tc, batch, wc), F32),
            pltpu.VMEM((2, batch, wc), F32),
        ],
        compiler_params=pltpu.CompilerParams(
            dimension_semantics=("arbitrary", "arbitrary"), vmem_limit_bytes=_vmem_limit(est)),
        name="lru",
    )(x_t, x_t, x_t, x_t, x_t, x_t, conv_w, conv_b, wg, bg, lam, h0)


def _merge_kernel(attn_ref, ga_ref, hf_ref, hb_ref, gr_ref, gm_ref, x_ref, mod_ref, gp_ref,
                  wa_ref, wl_ref, wo_ref, o_ref, *, d_model):
    def silu(ref):
        h = ref[...] * 0.5
        return h + h * jnp.tanh(h)

    a_in = attn_ref[...] * silu(ga_ref)
    y_attn = jnp.dot(a_in, wa_ref[...], preferred_element_type=F32)
    l_in = (hf_ref[...] + hb_ref[...]) * silu(gr_ref)
    y_lru = jnp.dot(l_in, wl_ref[...], preferred_element_type=F32)
    t_attn = jnp.tanh(0.5 * gm_ref[:, 0:d_model].astype(F32))
    t_lru = jnp.tanh(0.5 * gm_ref[:, d_model:].astype(F32))
    z = (0.5 * ((t_attn + 1.0) * y_attn + (t_lru + 1.0) * y_lru)).astype(BF16)
    y = jnp.dot(z, wo_ref[...], preferred_element_type=F32)
    ms = jnp.mean(y * y, axis=-1, keepdims=True)
    yn = y * lax.rsqrt(ms + NORM_EPS) * gp_ref[...]
    gate = mod_ref[:, 2 * d_model:]
    o_ref[...] = x_ref[...] + gate * yn


def _merge(attn, ga, hf, hb, gr, gm, x2, mod3, g_post, wa, wl, wo, rows_per_batch):
    n, d = x2.shape
    tm = MERGE_TM
    tiles_per_batch = rows_per_batch // tm
    row = pl.BlockSpec((tm, d), lambda i: (i, 0))
    trow = pl.BlockSpec((tm, d), lambda i: (i % tiles_per_batch, i // tiles_per_batch))
    wspec = pl.BlockSpec((d, d), lambda i: (0, 0), pipeline_mode=pl.Buffered(1))
    est = 3 * d * d * 2 + 2 * (5 * tm * d * 2 + tm * 2 * d * 2 + 2 * tm * d * 4) + 10 * tm * d * 4 + (4 << 20)
    return pl.pallas_call(
        functools.partial(_merge_kernel, d_model=d),
        grid=(n // tm,),
        in_specs=[
            row, row, trow, trow, row,
            pl.BlockSpec((tm, 2 * d), lambda i: (i, 0)),
            row,
            pl.BlockSpec((None, 1, 3 * d), lambda i: (i // tiles_per_batch, 0, 0)),
            pl.BlockSpec((1, d), lambda i: (0, 0)),
            wspec, wspec, wspec,
        ],
        out_specs=row,
        out_shape=jax.ShapeDtypeStruct((n, d), F32),
        compiler_params=pltpu.CompilerParams(
            dimension_semantics=("arbitrary",), vmem_limit_bytes=_vmem_limit(est)),
        name="merge",
    )(attn, ga, hf, hb, gr, gm, x2, mod3, g_post, wa, wl, wo)


def _rope_tables(n_tokens):
    t = np.arange(n_tokens)
    row = (t // GRID_W).astype(np.float64)
    col = (t % GRID_W).astype(np.float64)
    axis_dim = HEAD_DIM // 2
    inv_freq = ROPE_THETA ** (-np.arange(0, axis_dim, 2, dtype=np.float64) / axis_dim)
    d = np.arange(LANES) % HEAD_DIM
    freq = inv_freq[d % (axis_dim // 2)]
    pos = np.where((d < axis_dim)[None, :], row[:, None], col[:, None])
    ang = pos * freq[None, :]
    sign = np.where((d % axis_dim) < axis_dim // 2, -1.0, 1.0)
    cos = np.cos(ang)
    sin = np.sin(ang) * sign[None, :]
    qs = (HEAD_DIM ** -0.5) * math.log2(math.e)
    return tuple(jnp.asarray(a, F32) for a in (cos * qs, sin * qs, cos, sin))


def _gate_weights(w_a, w_x, b_a, b_x, wc):
    per = wc // (w_a.shape[-1])
    n_slab = LRU_BLOCKS // per
    eye = jnp.eye(per, dtype=F32)

    def slabs(w):
        w5 = w.reshape(2, n_slab, per, w.shape[-2], w.shape[-1])
        bd = jnp.einsum("dsnij,nm->dsnimj", w5, eye)
        return bd.reshape(2, n_slab, wc, wc)

    wg = (0.5 * jnp.concatenate([slabs(w_a), slabs(w_x)], axis=-1)).astype(BF16)
    bg = 0.5 * jnp.concatenate([b_a.reshape(2, n_slab, 1, wc), b_x.reshape(2, n_slab, 1, wc)], axis=-1)
    return wg, bg


def kernel(x, c, ctx, c_ctx, w_mod, b_mod, g_pre, g_post, w_in, lambda_q1, lambda_k1, lambda_q2, lambda_k2,
           g_subln, w_attn_out, conv_w, conv_b, w_rg_a, b_rg_a, w_rg_x, b_rg_x, lru_lambda, w_lru_out, w_out):
    batch, n_lat, d = x.shape
    n_ctx = ctx.shape[1]
    assert w_mod.shape[0] == 1, "single-layer block"
    assert d == HEADS * V_DIM and n_lat % PROJ_TM == 0 and n_lat % MERGE_TM == 0 and n_ctx % BF16_ROWS == 0

    c_all = jnp.zeros((MOD_ROWS, d), F32).at[:batch].set(c).at[batch].set(c_ctx)
    vec = lambda a: a[0].reshape(1, HEAD_DIM)
    mod, lam_row = _modulation(c_all, w_mod[0], b_mod[0].reshape(1, -1),
                               vec(lambda_q1), vec(lambda_k1), vec(lambda_q2), vec(lambda_k2))
    mod3 = mod.reshape(MOD_ROWS, 1, 3 * d)

    w_bf = w_in[0].astype(BF16)
    tables = _rope_tables(n_lat)
    g_pre2 = g_pre[0].reshape(1, d)
    x2 = x.reshape(batch * n_lat, d)
    lat_groups = ((d, "q", 0, False), (d, "k", d, False), (d, None, 2 * d, False), (d, None, 3 * d, False),
                  (d, None, 4 * d, True), (d, None, 5 * d, False), (2 * d, None, 6 * d, False))
    q, k, v, ga, xr, gr, gm = _project(
        x2, mod3, g_pre2, w_bf, tables, lat_groups, n_lat, lambda i, tpb: i // tpb)

    ctx_groups = ((d, None, d, False), (d, None, 2 * d, False), (d, None, 4 * d, True))
    kc, vc, xrc = _project(
        ctx.reshape(batch * n_ctx, d), mod3, g_pre2, w_bf, tables, ctx_groups, n_ctx,
        lambda i, tpb: batch)

    attn = _attention(q, kc, k, vc, v, lam_row, g_subln[0].reshape(1, V_DIM), batch, n_lat, n_ctx)

    wg, bg = _gate_weights(w_rg_a[0], w_rg_x[0], b_rg_a[0], b_rg_x[0], LRU_WC)
    lam = lru_lambda[0].reshape(2, 1, d)
    cb = conv_b[0].reshape(1, d)
    zeros = jnp.zeros((2, batch, d), F32)
    _, _, h_ctx = _scan(xrc.reshape(n_ctx, batch, d), conv_w[0], cb, wg, bg, lam, zeros)
    hf, hb, _ = _scan(xr.reshape(n_lat, batch, d), conv_w[0], cb, wg, bg, lam, h_ctx)

    out = _merge(attn, ga, hf.reshape(n_lat, batch * d), hb.reshape(n_lat, batch * d), gr, gm, x2, mod3,
                 g_post[0].reshape(1, d), w_attn_out[0].astype(BF16), w_lru_out[0].astype(BF16),
                 w_out[0].astype(BF16), n_lat)
    return out.reshape(batch, n_lat, d)
```

```python
import functools
import math

import jax
import jax.numpy as jnp
import numpy as np
from jax import lax
from jax.experimental import pallas as pl
from jax.experimental.pallas import tpu as pltpu

F32 = jnp.float32
BF16 = jnp.bfloat16

LANES = 128
SUBLANES = 8
BF16_ROWS = 16
VMEM_BYTES = 64 * 1024 * 1024

GRID_W = 64
HEADS = 8
HEAD_DIM = 64
V_DIM = 2 * HEAD_DIM
CONV_WIDTH = 4
LRU_C = 8.0
ROPE_THETA = 10000.0
NORM_EPS = 1e-6
LAMBDA_INIT = 0.8 - 0.6 * math.exp(-0.3 * 0)

MOD_ROWS = 24
PROJ_TM = 512
MERGE_TM = 512
ATTN_ROWS = 128
LRU_TC = 128
LRU_WC = 256
LRU_GATE_ROWS = 512


def _sigmoid(x):
    return 0.5 * jnp.tanh(0.5 * x) + 0.5


def _vmem_limit(nbytes):
    return int(min(nbytes, VMEM_BYTES - 8 * 1024 * 1024))


def _mod_kernel(c_ref, w_ref, b_ref, lq1_ref, lk1_ref, lq2_ref, lk2_ref, mod_ref, lam_ref):
    c = c_ref[...]
    s = c * _sigmoid(c)
    mod_ref[...] = jnp.dot(s, w_ref[...], precision=lax.Precision.HIGHEST,
                           preferred_element_type=F32) + b_ref[...]
    s1 = jnp.sum(lq1_ref[...] * lk1_ref[...], axis=-1, keepdims=True)
    s2 = jnp.sum(lq2_ref[...] * lk2_ref[...], axis=-1, keepdims=True)
    lam = jnp.exp(s1) - jnp.exp(s2) + LAMBDA_INIT
    lam_ref[...] = jnp.broadcast_to(lam, lam_ref.shape)


def _modulation(c_all, w_mod, b_mod, lq1, lk1, lq2, lk2):
    d = c_all.shape[1]
    n_out = w_mod.shape[1]
    tn = n_out // 2
    vec = pl.BlockSpec((1, HEAD_DIM), lambda j: (0, 0))
    return pl.pallas_call(
        _mod_kernel,
        grid=(n_out // tn,),
        in_specs=[
            pl.BlockSpec((MOD_ROWS, d), lambda j: (0, 0)),
            pl.BlockSpec((d, tn), lambda j: (0, j)),
            pl.BlockSpec((1, tn), lambda j: (0, j)),
            vec, vec, vec, vec,
        ],
        out_specs=[
            pl.BlockSpec((MOD_ROWS, tn), lambda j: (0, j)),
            pl.BlockSpec((1, LANES), lambda j: (0, 0)),
        ],
        out_shape=[
            jax.ShapeDtypeStruct((MOD_ROWS, n_out), F32),
            jax.ShapeDtypeStruct((1, LANES), F32),
        ],
        name="mod",
    )(c_all, w_mod, b_mod, lq1, lk1, lq2, lk2)


def _rope_slab(xs, cos, sin, low_half):
    up = pltpu.roll(xs, LANES - 16, 1)
    dn = pltpu.roll(xs, 16, 1)
    return xs * cos + jnp.where(low_half, up, dn) * sin


def _proj_kernel(x_ref, mod_ref, g_ref, qcos_ref, qsin_ref, kcos_ref, ksin_ref, *w_and_out_refs,
                 d_model, groups):
    w_refs, out_refs = w_and_out_refs[:len(groups)], w_and_out_refs[len(groups):]
    x = x_ref[...]
    ms = jnp.mean(x * x, axis=-1, keepdims=True)
    y = x * lax.rsqrt(ms + NORM_EPS) * g_ref[...]
    shift = mod_ref[:, 0:d_model]
    scale = mod_ref[:, d_model:2 * d_model]
    h = (y * (1.0 + scale) + shift).astype(BF16)
    lane = lax.broadcasted_iota(jnp.int32, (x.shape[0], LANES), 1)
    low_half = (lane & 16) == 0
    for (width, rope, _, _), w_ref, o_ref in zip(groups, w_refs, out_refs):
        r = jnp.dot(h, w_ref[...], preferred_element_type=F32)
        if rope is None:
            o_ref[...] = r.astype(o_ref.dtype)
        else:
            cos_ref, sin_ref = (qcos_ref, qsin_ref) if rope == "q" else (kcos_ref, ksin_ref)
            cos = cos_ref[...]
            sin = sin_ref[...]
            for s in range(width // LANES):
                sl = slice(s * LANES, (s + 1) * LANES)
                o_ref[:, sl] = _rope_slab(r[:, sl], cos, sin, low_half).astype(o_ref.dtype)


def _project(x2, mod3, g_pre, w, tables, groups, rows_per_batch, mod_row_of_tile):
    n, d = x2.shape
    tm = min(PROJ_TM, rows_per_batch)
    tiles_per_batch = rows_per_batch // tm
    n_batches = n // rows_per_batch
    tab = pl.BlockSpec((tm, LANES), lambda i: (i % tiles_per_batch, 0))
    out_cols = sum(g[0] for g in groups)

    def out_spec(width, time_major):
        if time_major:
            return pl.BlockSpec((tm, width), lambda i: (i % tiles_per_batch, i // tiles_per_batch))
        return pl.BlockSpec((tm, width), lambda i: (i, 0))

    def out_shape(width, time_major):
        shape = (rows_per_batch, n_batches * width) if time_major else (n, width)
        return jax.ShapeDtypeStruct(shape, BF16)

    est = (d * out_cols * 2 + 2 * tm * d * 4 + 2 * tm * out_cols * 2 + 6 * tm * d * 4
           + 8 * tm * LANES * 4 + (4 << 20))

    def w_cols(width, offset):
        assert offset % width == 0
        return pl.BlockSpec((d, width), lambda i: (0, offset // width), pipeline_mode=pl.Buffered(1))

    return pl.pallas_call(
        functools.partial(_proj_kernel, d_model=d, groups=groups),
        grid=(n // tm,),
        in_specs=[
            pl.BlockSpec((tm, d), lambda i: (i, 0)),
            pl.BlockSpec((None, 1, 3 * d), lambda i: (mod_row_of_tile(i, tiles_per_batch), 0, 0)),
            pl.BlockSpec((1, d), lambda i: (0, 0)),
            tab, tab, tab, tab,
        ] + [w_cols(wd, off) for wd, _, off, _ in groups],
        out_specs=[out_spec(wd, tmaj) for wd, _, _, tmaj in groups],
        out_shape=[out_shape(wd, tmaj) for wd, _, _, tmaj in groups],
        compiler_params=pltpu.CompilerParams(
            dimension_semantics=("arbitrary",), vmem_limit_bytes=_vmem_limit(est)),
        name="proj",
    )(x2, mod3, g_pre, *tables, *([w] * len(groups)))


def _attn_kernel(q_ref, kc_ref, k_ref, vc_ref, v_ref, lam_ref, g_ref, o_ref, kall_ref, vext_ref,
                 s0_ref, s1_ref, *, n_ctx, rows):
    n_sub = q_ref.shape[0] // rows
    kall_ref[0:n_ctx, :] = kc_ref[...]
    kall_ref[n_ctx:, :] = k_ref[...]
    vext_ref[0:n_ctx, 0:V_DIM] = vc_ref[...]
    vext_ref[n_ctx:, 0:V_DIM] = v_ref[...]
    vext_ref[:, V_DIM:] = jnp.ones((vext_ref.shape[0], V_DIM), BF16)
    lane = lax.broadcasted_iota(jnp.int32, (rows, V_DIM), 1)
    first_half = lane < HEAD_DIM

    def scores(r, s_ref):
        q = q_ref[pl.ds(pl.multiple_of(r * rows, rows), rows), :]
        zero = jnp.zeros_like(q)
        q2 = jnp.concatenate([jnp.where(first_half, q, zero), jnp.where(first_half, zero, q)], axis=0)
        s_ref[...] = lax.dot_general(q2, kall_ref[...], (((1,), (1,)), ((), ())),
                                     preferred_element_type=F32)

    def attend(r, s_ref):
        s = s_ref[...]
        m = jnp.max(s, axis=-1, keepdims=True)
        e = jnp.exp2(s - m).astype(BF16)
        acc = jnp.dot(e, vext_ref[...], preferred_element_type=F32)
        oj = acc[:, 0:V_DIM] / acc[:, V_DIM:]
        o = oj[0:rows] - lam_ref[...] * oj[rows:]
        ms = jnp.mean(o * o, axis=-1, keepdims=True)
        o = o * lax.rsqrt(ms + NORM_EPS) * g_ref[...] * (1.0 - LAMBDA_INIT)
        o_ref[pl.ds(pl.multiple_of(r * rows, rows), rows), :] = o.astype(o_ref.dtype)

    scores(0, s0_ref)

    def pair(p, carry):
        r = 2 * p
        scores(r + 1, s1_ref)
        attend(r, s0_ref)
        scores(r + 2, s0_ref)
        attend(r + 1, s1_ref)
        return carry

    lax.fori_loop(0, n_sub // 2 - 1, pair, 0, unroll=True)
    scores(n_sub - 1, s1_ref)
    attend(n_sub - 2, s0_ref)
    attend(n_sub - 1, s1_ref)


def _attention(q, kc, k, vc, v, lam_row, g_subln, batch, n_lat, n_ctx):
    rows = ATTN_ROWS
    assert n_lat % (2 * rows) == 0
    n_keys = n_ctx + n_lat
    est = (2 * (n_keys * V_DIM * 2 * 2) + n_keys * V_DIM * 2 + n_keys * 2 * V_DIM * 2
           + 4 * n_lat * V_DIM * 2 + 2 * (2 * rows * n_keys * 4) + 4 * (2 * rows * n_keys * 4) + (8 << 20))
    return pl.pallas_call(
        functools.partial(_attn_kernel, n_ctx=n_ctx, rows=rows),
        grid=(batch, HEADS),
        in_specs=[
            pl.BlockSpec((n_lat, V_DIM), lambda b, h: (b, h)),
            pl.BlockSpec((n_ctx, V_DIM), lambda b, h: (b, h)),
            pl.BlockSpec((n_lat, V_DIM), lambda b, h: (b, h)),
            pl.BlockSpec((n_ctx, V_DIM), lambda b, h: (b, h)),
            pl.BlockSpec((n_lat, V_DIM), lambda b, h: (b, h)),
            pl.BlockSpec((1, LANES), lambda b, h: (0, 0)),
            pl.BlockSpec((1, V_DIM), lambda b, h: (0, 0)),
        ],
        out_specs=pl.BlockSpec((n_lat, V_DIM), lambda b, h: (b, h)),
        out_shape=jax.ShapeDtypeStruct((batch * n_lat, HEADS * V_DIM), BF16),
        scratch_shapes=[
            pltpu.VMEM((n_keys, V_DIM), BF16),
            pltpu.VMEM((n_keys, 2 * V_DIM), BF16),
            pltpu.VMEM((2 * rows, n_keys), F32),
            pltpu.VMEM((2 * rows, n_keys), F32),
        ],
        compiler_params=pltpu.CompilerParams(
            dimension_semantics=("arbitrary", "arbitrary"),
            vmem_limit_bytes=_vmem_limit(est)),
        name="attn",
    )(q, kc, k, vc, v, lam_row, g_subln)


def _scan_kernel(xf_ref, xfp_ref, xfn_ref, xb_ref, xbp_ref, xbn_ref, cw_ref, cb_ref, wg_ref, bg_ref,
                 lam_ref, h0_ref, hf_ref, hb_ref, hfin_ref, xe_ref, wg_scr, a_ref, u_ref, st_ref,
                 *, batch, tc, wc):
    i = pl.program_id(1)
    n = pl.num_programs(1)
    t_grp = LRU_GATE_ROWS // batch
    n_blk, blk = wg_ref.shape[2], wg_ref.shape[3]

    @pl.when(i == 0)
    def _():
        st_ref[...] = h0_ref[...]
        wg_scr[...] = jnp.zeros(wg_scr.shape, BF16)
        for d in range(2):
            for gate in range(2):
                for m in range(n_blk):
                    wg_scr[d, m * blk:(m + 1) * blk, gate * wc + m * blk:gate * wc + (m + 1) * blk] = (
                        wg_ref[d, gate, m].astype(BF16))

    half_taps = [0.5 * cw_ref[j:j + 1, :] for j in range(CONV_WIDTH)]
    half_cb = 0.5 * cb_ref[...]

    def coeffs(d, cur_ref, prev_ref, next_ref, chunk):
        xe_ref[d, 1:tc + 1] = cur_ref[...].astype(F32)
        xe_ref[d, 0:1] = jnp.where(chunk == 0, 0.0, prev_ref[...].astype(F32))
        xe_ref[d, tc + 1:tc + 3] = jnp.where(chunk == n - 1, 0.0, next_ref[...].astype(F32))
        half_cl = (-0.5 * LRU_C * math.log2(math.e)) * jax.nn.softplus(-lam_ref[d])
        half_bg = 0.5 * bg_ref[d]
        for t0 in range(0, tc, t_grp):
            xh = half_cb
            for j in range(CONV_WIDTH):
                xh = xh + xe_ref[d, t0 + j:t0 + j + t_grp] * half_taps[j]
            xh = xh.reshape(t_grp * batch, wc)
            z = jnp.dot(xh.astype(BF16), wg_scr[d], preferred_element_type=F32) + half_bg
            a = jnp.exp2(half_cl * jnp.tanh(z[:, 0:wc]) + half_cl)
            y = 1.0 - a * a
            mult = y * lax.rsqrt(jnp.maximum(y, 1e-30))
            u = (mult * xh) * (jnp.tanh(z[:, wc:]) + 1.0)
            a_ref[d, t0:t0 + t_grp] = a.reshape(t_grp, batch, wc)
            u_ref[d, t0:t0 + t_grp] = u.reshape(t_grp, batch, wc)

    coeffs(0, xf_ref, xfp_ref, xfn_ref, i)
    coeffs(1, xb_ref, xbp_ref, xbn_ref, n - 1 - i)

    def scan_step(t, hs):
        tb = tc - 1 - t
        h_f = a_ref[0, t] * hs[0] + u_ref[0, t]
        hf_ref[t] = h_f.astype(hf_ref.dtype)
        h_b = a_ref[1, tb] * hs[1] + u_ref[1, tb]
        hb_ref[tb] = h_b.astype(hb_ref.dtype)
        return h_f, h_b

    h_f, h_b = lax.fori_loop(0, tc, scan_step, (st_ref[0], st_ref[1]), unroll=8)
    st_ref[0] = h_f
    st_ref[1] = h_b

    @pl.when(i == n - 1)
    def _():
        hfin_ref[...] = st_ref[...]


def _scan(x_t, conv_w, conv_b, wg, bg, lam, h0):
    length, batch, width = x_t.shape
    tc, wc = LRU_TC, LRU_WC
    n = length // tc
    blk = wg.shape[-1]
    assert tc % 2 == 0 and LRU_GATE_ROWS % batch == 0 and tc % (LRU_GATE_ROWS // batch) == 0
    assert wc % blk == 0 and wg.shape[2] * blk == width

    def chunk(fwd, i):
        return i if fwd else n - 1 - i

    def cur(fwd):
        return pl.BlockSpec((tc, batch, wc), lambda s, i: (chunk(fwd, i), 0, s))

    def prev(fwd):
        return pl.BlockSpec((1, batch, wc), lambda s, i: (jnp.maximum(chunk(fwd, i) * tc - 1, 0), 0, s))

    def nxt(fwd):
        return pl.BlockSpec((2, batch, wc), lambda s, i: (
            jnp.minimum((chunk(fwd, i) + 1) * (tc // 2), length // 2 - 1), 0, s))

    chunk_f32 = 2 * tc * batch * wc * 4
    est = (2 * (tc + 3) * batch * wc * 4 + 2 * chunk_f32 + 8 * tc * batch * wc * 2
           + 8 * LRU_GATE_ROWS * 2 * wc * 4 + (8 << 20))
    return pl.pallas_call(
        functools.partial(_scan_kernel, batch=batch, tc=tc, wc=wc),
        grid=(width // wc, n),
        in_specs=[
            cur(True), prev(True), nxt(True), cur(False), prev(False), nxt(False),
            pl.BlockSpec((CONV_WIDTH, wc), lambda s, i: (0, s)),
            pl.BlockSpec((1, wc), lambda s, i: (0, s)),
            pl.BlockSpec((2, 2, wc // blk, blk, blk), lambda s, i: (0, 0, s, 0, 0)),
            pl.BlockSpec((2, None, 1, 2 * wc), lambda s, i: (0, s, 0, 0)),
            pl.BlockSpec((2, 1, wc), lambda s, i: (0, 0, s)),
            pl.BlockSpec((2, batch, wc), lambda s, i: (0, 0, s)),
        ],
        out_specs=[cur(True), cur(False), pl.BlockSpec((2, batch, wc), lambda s, i: (0, 0, s))],
        out_shape=[
            jax.ShapeDtypeStruct((length, batch, width), BF16),
            jax.ShapeDtypeStruct((length, batch, width), BF16),
            jax.ShapeDtypeStruct((2, batch, width), F32),
        ],
        scratch_shapes=[
            pltpu.VMEM((2, tc + 3, batch, wc), F32),
            pltpu.VMEM((2, wc, 2 * wc), BF16),
            pltpu.VMEM((2, tc, batch, wc), F32),
            pltpu.VMEM((2, tc, batch, wc), F32),
            pltpu.VMEM((2, batch, wc), F32),
        ],
        compiler_params=pltpu.CompilerParams(
            dimension_semantics=("arbitrary", "arbitrary"), vmem_limit_bytes=_vmem_limit(est)),
        name="lru",
    )(x_t, x_t, x_t, x_t, x_t, x_t, conv_w, conv_b, wg, bg, lam, h0)


def _merge_kernel(attn_ref, ga_ref, hf_ref, hb_ref, gr_ref, gm_ref, x_ref, mod_ref, gp_ref,
                  wa_ref, wl_ref, wo_ref, o_ref, *, d_model):
    def silu(ref):
        h = ref[...] * 0.5
        return h + h * jnp.tanh(h)

    a_in = attn_ref[...] * silu(ga_ref)
    y_attn = jnp.dot(a_in, wa_ref[...], preferred_element_type=F32)
    l_in = (hf_ref[...] + hb_ref[...]) * silu(gr_ref)
    y_lru = jnp.dot(l_in, wl_ref[...], preferred_element_type=F32)
    t_attn = jnp.tanh(0.5 * gm_ref[:, 0:d_model].astype(F32))
    t_lru = jnp.tanh(0.5 * gm_ref[:, d_model:].astype(F32))
    z = (0.5 * ((t_attn + 1.0) * y_attn + (t_lru + 1.0) * y_lru)).astype(BF16)
    y = jnp.dot(z, wo_ref[...], preferred_element_type=F32)
    ms = jnp.mean(y * y, axis=-1, keepdims=True)
    yn = y * lax.rsqrt(ms + NORM_EPS) * gp_ref[...]
    gate = mod_ref[:, 2 * d_model:]
    o_ref[...] = x_ref[...] + gate * yn


def _merge(attn, ga, hf, hb, gr, gm, x2, mod3, g_post, wa, wl, wo, rows_per_batch):
    n, d = x2.shape
    tm = MERGE_TM
    tiles_per_batch = rows_per_batch // tm
    row = pl.BlockSpec((tm, d), lambda i: (i, 0))
    trow = pl.BlockSpec((tm, d), lambda i: (i % tiles_per_batch, i // tiles_per_batch))
    wspec = pl.BlockSpec((d, d), lambda i: (0, 0), pipeline_mode=pl.Buffered(1))
    est = 3 * d * d * 2 + 2 * (5 * tm * d * 2 + tm * 2 * d * 2 + 2 * tm * d * 4) + 10 * tm * d * 4 + (4 << 20)
    return pl.pallas_call(
        functools.partial(_merge_kernel, d_model=d),
        grid=(n // tm,),
        in_specs=[
            row, row, trow, trow, row,
            pl.BlockSpec((tm, 2 * d), lambda i: (i, 0)),
            row,
            pl.BlockSpec((None, 1, 3 * d), lambda i: (i // tiles_per_batch, 0, 0)),
            pl.BlockSpec((1, d), lambda i: (0, 0)),
            wspec, wspec, wspec,
        ],
        out_specs=row,
        out_shape=jax.ShapeDtypeStruct((n, d), F32),
        compiler_params=pltpu.CompilerParams(
            dimension_semantics=("arbitrary",), vmem_limit_bytes=_vmem_limit(est)),
        name="merge",
    )(attn, ga, hf, hb, gr, gm, x2, mod3, g_post, wa, wl, wo)


def _rope_tables(n_tokens):
    t = np.arange(n_tokens)
    row = (t // GRID_W).astype(np.float64)
    col = (t % GRID_W).astype(np.float64)
    axis_dim = HEAD_DIM // 2
    inv_freq = ROPE_THETA ** (-np.arange(0, axis_dim, 2, dtype=np.float64) / axis_dim)
    d = np.arange(LANES) % HEAD_DIM
    freq = inv_freq[d % (axis_dim // 2)]
    pos = np.where((d < axis_dim)[None, :], row[:, None], col[:, None])
    ang = pos * freq[None, :]
    sign = np.where((d % axis_dim) < axis_dim // 2, -1.0, 1.0)
    cos = np.cos(ang)
    sin = np.sin(ang) * sign[None, :]
    qs = (HEAD_DIM ** -0.5) * math.log2(math.e)
    return tuple(jnp.asarray(a, F32) for a in (cos * qs, sin * qs, cos, sin))


def _gate_params(w_a, w_x, b_a, b_x, wc):
    n_slab = b_a.shape[-1] // wc
    wg = jnp.stack([w_a, w_x], axis=1)
    bg = jnp.concatenate([b_a.reshape(2, n_slab, 1, wc), b_x.reshape(2, n_slab, 1, wc)], axis=-1)
    return wg, bg


def kernel(x, c, ctx, c_ctx, w_mod, b_mod, g_pre, g_post, w_in, lambda_q1, lambda_k1, lambda_q2, lambda_k2,
           g_subln, w_attn_out, conv_w, conv_b, w_rg_a, b_rg_a, w_rg_x, b_rg_x, lru_lambda, w_lru_out, w_out):
    batch, n_lat, d = x.shape
    n_ctx = ctx.shape[1]
    assert w_mod.shape[0] == 1, "single-layer block"
    assert d == HEADS * V_DIM and n_lat % PROJ_TM == 0 and n_lat % MERGE_TM == 0 and n_ctx % BF16_ROWS == 0

    c_all = jnp.zeros((MOD_ROWS, d), F32).at[:batch].set(c).at[batch].set(c_ctx)
    vec = lambda a: a[0].reshape(1, HEAD_DIM)
    mod, lam_row = _modulation(c_all, w_mod[0], b_mod[0].reshape(1, -1),
                               vec(lambda_q1), vec(lambda_k1), vec(lambda_q2), vec(lambda_k2))
    mod3 = mod.reshape(MOD_ROWS, 1, 3 * d)

    w_bf = w_in[0].astype(BF16)
    tables = _rope_tables(n_lat)
    g_pre2 = g_pre[0].reshape(1, d)
    x2 = x.reshape(batch * n_lat, d)
    lat_groups = ((d, "q", 0, False), (d, "k", d, False), (d, None, 2 * d, False), (d, None, 3 * d, False),
                  (d, None, 4 * d, True), (d, None, 5 * d, False), (2 * d, None, 6 * d, False))
    q, k, v, ga, xr, gr, gm = _project(
        x2, mod3, g_pre2, w_bf, tables, lat_groups, n_lat, lambda i, tpb: i // tpb)

    ctx_groups = ((d, None, d, False), (d, None, 2 * d, False), (d, None, 4 * d, True))
    kc, vc, xrc = _project(
        ctx.reshape(batch * n_ctx, d), mod3, g_pre2, w_bf, tables, ctx_groups, n_ctx,
        lambda i, tpb: batch)

    attn = _attention(q, kc, k, vc, v, lam_row, g_subln[0].reshape(1, V_DIM), batch, n_lat, n_ctx)

    wg, bg = _gate_params(w_rg_a[0], w_rg_x[0], b_rg_a[0], b_rg_x[0], LRU_WC)
    lam = lru_lambda[0].reshape(2, 1, d)
    cb = conv_b[0].reshape(1, d)
    zeros = jnp.zeros((2, batch, d), F32)
    _, _, h_ctx = _scan(xrc.reshape(n_ctx, batch, d), conv_w[0], cb, wg, bg, lam, zeros)
    hf, hb, _ = _scan(xr.reshape(n_lat, batch, d), conv_w[0], cb, wg, bg, lam, h_ctx)

    out = _merge(attn, ga, hf.reshape(n_lat, batch * d), hb.reshape(n_lat, batch * d), gr, gm, x2, mod3,
                 g_post[0].reshape(1, d), w_attn_out[0].astype(BF16), w_lru_out[0].astype(BF16),
                 w_out[0].astype(BF16), n_lat)
    return out.reshape(batch, n_lat, d)
```

```python
import functools
import math

import jax
import jax.numpy as jnp
import numpy as np
from jax import lax
from jax.experimental import pallas as pl
from jax.experimental.pallas import tpu as pltpu

F32 = jnp.float32
BF16 = jnp.bfloat16

LANES = 128
SUBLANES = 8
BF16_ROWS = 16
VMEM_BYTES = 64 * 1024 * 1024

GRID_W = 64
HEADS = 8
HEAD_DIM = 64
V_DIM = 2 * HEAD_DIM
CONV_WIDTH = 4
LRU_C = 8.0
ROPE_THETA = 10000.0
NORM_EPS = 1e-6
LAMBDA_INIT = 0.8 - 0.6 * math.exp(-0.3 * 0)

MOD_ROWS = 24
PROJ_TM = 512
MERGE_TM = 512
ATTN_ROWS = 128
LRU_TC = 256
LRU_WC = 256
LRU_GATE_ROWS = 512


def _sigmoid(x):
    return 0.5 * jnp.tanh(0.5 * x) + 0.5


def _vmem_limit(nbytes):
    return int(min(nbytes, VMEM_BYTES - 8 * 1024 * 1024))


def _mod_kernel(c_ref, w_ref, b_ref, lq1_ref, lk1_ref, lq2_ref, lk2_ref, mod_ref, lam_ref):
    c = c_ref[...]
    s = c * _sigmoid(c)
    mod_ref[...] = jnp.dot(s, w_ref[...], precision=lax.Precision.HIGHEST,
                           preferred_element_type=F32) + b_ref[...]
    s1 = jnp.sum(lq1_ref[...] * lk1_ref[...], axis=-1, keepdims=True)
    s2 = jnp.sum(lq2_ref[...] * lk2_ref[...], axis=-1, keepdims=True)
    lam = jnp.exp(s1) - jnp.exp(s2) + LAMBDA_INIT
    lam_ref[...] = jnp.broadcast_to(lam, lam_ref.shape)


def _modulation(c_all, w_mod, b_mod, lq1, lk1, lq2, lk2):
    d = c_all.shape[1]
    n_out = w_mod.shape[1]
    tn = n_out // 2
    vec = pl.BlockSpec((1, HEAD_DIM), lambda j: (0, 0))
    return pl.pallas_call(
        _mod_kernel,
        grid=(n_out // tn,),
        in_specs=[
            pl.BlockSpec((MOD_ROWS, d), lambda j: (0, 0)),
            pl.BlockSpec((d, tn), lambda j: (0, j)),
            pl.BlockSpec((1, tn), lambda j: (0, j)),
            vec, vec, vec, vec,
        ],
        out_specs=[
            pl.BlockSpec((MOD_ROWS, tn), lambda j: (0, j)),
            pl.BlockSpec((1, LANES), lambda j: (0, 0)),
        ],
        out_shape=[
            jax.ShapeDtypeStruct((MOD_ROWS, n_out), F32),
            jax.ShapeDtypeStruct((1, LANES), F32),
        ],
        name="mod",
    )(c_all, w_mod, b_mod, lq1, lk1, lq2, lk2)


def _rope_slab(xs, cos, sin, low_half):
    up = pltpu.roll(xs, LANES - 16, 1)
    dn = pltpu.roll(xs, 16, 1)
    return xs * cos + jnp.where(low_half, up, dn) * sin


def _proj_kernel(x_ref, mod_ref, g_ref, qcos_ref, qsin_ref, kcos_ref, ksin_ref, *w_and_out_refs,
                 d_model, groups):
    w_refs, out_refs = w_and_out_refs[:len(groups)], w_and_out_refs[len(groups):]
    x = x_ref[...]
    ms = jnp.mean(x * x, axis=-1, keepdims=True)
    y = x * lax.rsqrt(ms + NORM_EPS) * g_ref[...]
    shift = mod_ref[:, 0:d_model]
    scale = mod_ref[:, d_model:2 * d_model]
    h = (y * (1.0 + scale) + shift).astype(BF16)
    lane = lax.broadcasted_iota(jnp.int32, (x.shape[0], LANES), 1)
    low_half = (lane & 16) == 0
    for (width, rope, _, _), w_ref, o_ref in zip(groups, w_refs, out_refs):
        r = jnp.dot(h, w_ref[...], preferred_element_type=F32)
        if rope is None:
            o_ref[...] = r.astype(o_ref.dtype)
        else:
            cos_ref, sin_ref = (qcos_ref, qsin_ref) if rope == "q" else (kcos_ref, ksin_ref)
            cos = cos_ref[...]
            sin = sin_ref[...]
            for s in range(width // LANES):
                sl = slice(s * LANES, (s + 1) * LANES)
                o_ref[:, sl] = _rope_slab(r[:, sl], cos, sin, low_half).astype(o_ref.dtype)


def _project(x2, mod3, g_pre, w, tables, groups, rows_per_batch, mod_row_of_tile):
    n, d = x2.shape
    tm = min(PROJ_TM, rows_per_batch)
    tiles_per_batch = rows_per_batch // tm
    n_batches = n // rows_per_batch
    tab = pl.BlockSpec((tm, LANES), lambda i: (i % tiles_per_batch, 0))
    out_cols = sum(g[0] for g in groups)

    def out_spec(width, time_major):
        if time_major:
            return pl.BlockSpec((tm, width), lambda i: (i % tiles_per_batch, i // tiles_per_batch))
        return pl.BlockSpec((tm, width), lambda i: (i, 0))

    def out_shape(width, time_major):
        shape = (rows_per_batch, n_batches * width) if time_major else (n, width)
        return jax.ShapeDtypeStruct(shape, BF16)

    est = (d * out_cols * 2 + 2 * tm * d * 4 + 2 * tm * out_cols * 2 + 6 * tm * d * 4
           + 8 * tm * LANES * 4 + (4 << 20))

    def w_cols(width, offset):
        assert offset % width == 0
        return pl.BlockSpec((d, width), lambda i: (0, offset // width), pipeline_mode=pl.Buffered(1))

    return pl.pallas_call(
        functools.partial(_proj_kernel, d_model=d, groups=groups),
        grid=(n // tm,),
        in_specs=[
            pl.BlockSpec((tm, d), lambda i: (i, 0)),
            pl.BlockSpec((None, 1, 3 * d), lambda i: (mod_row_of_tile(i, tiles_per_batch), 0, 0)),
            pl.BlockSpec((1, d), lambda i: (0, 0)),
            tab, tab, tab, tab,
        ] + [w_cols(wd, off) for wd, _, off, _ in groups],
        out_specs=[out_spec(wd, tmaj) for wd, _, _, tmaj in groups],
        out_shape=[out_shape(wd, tmaj) for wd, _, _, tmaj in groups],
        compiler_params=pltpu.CompilerParams(
            dimension_semantics=("arbitrary",), vmem_limit_bytes=_vmem_limit(est)),
        name="proj",
    )(x2, mod3, g_pre, *tables, *([w] * len(groups)))


def _attn_kernel(q_ref, kc_ref, k_ref, vc_ref, v_ref, lam_ref, g_ref, o_ref, kall_ref, vext_ref,
                 s0_ref, s1_ref, *, n_ctx, rows):
    n_sub = q_ref.shape[0] // rows
    kall_ref[0:n_ctx, :] = kc_ref[...]
    kall_ref[n_ctx:, :] = k_ref[...]
    vext_ref[0:n_ctx, 0:V_DIM] = vc_ref[...]
    vext_ref[n_ctx:, 0:V_DIM] = v_ref[...]
    vext_ref[:, V_DIM:] = jnp.ones((vext_ref.shape[0], V_DIM), BF16)
    lane = lax.broadcasted_iota(jnp.int32, (rows, V_DIM), 1)
    first_half = lane < HEAD_DIM

    def scores(r, s_ref):
        q = q_ref[pl.ds(pl.multiple_of(r * rows, rows), rows), :]
        zero = jnp.zeros_like(q)
        q2 = jnp.concatenate([jnp.where(first_half, q, zero), jnp.where(first_half, zero, q)], axis=0)
        s_ref[...] = lax.dot_general(q2, kall_ref[...], (((1,), (1,)), ((), ())),
                                     preferred_element_type=F32)

    def attend(r, s_ref):
        s = s_ref[...]
        m = jnp.max(s, axis=-1, keepdims=True)
        e = jnp.exp2(s - m).astype(BF16)
        acc = jnp.dot(e, vext_ref[...], preferred_element_type=F32)
        oj = acc[:, 0:V_DIM] / acc[:, V_DIM:]
        o = oj[0:rows] - lam_ref[...] * oj[rows:]
        ms = jnp.mean(o * o, axis=-1, keepdims=True)
        o = o * lax.rsqrt(ms + NORM_EPS) * g_ref[...] * (1.0 - LAMBDA_INIT)
        o_ref[pl.ds(pl.multiple_of(r * rows, rows), rows), :] = o.astype(o_ref.dtype)

    scores(0, s0_ref)

    def pair(p, carry):
        r = 2 * p
        scores(r + 1, s1_ref)
        attend(r, s0_ref)
        scores(r + 2, s0_ref)
        attend(r + 1, s1_ref)
        return carry

    lax.fori_loop(0, n_sub // 2 - 1, pair, 0, unroll=True)
    scores(n_sub - 1, s1_ref)
    attend(n_sub - 2, s0_ref)
    attend(n_sub - 1, s1_ref)


def _attention(q, kc, k, vc, v, lam_row, g_subln, batch, n_lat, n_ctx):
    rows = ATTN_ROWS
    assert n_lat % (2 * rows) == 0
    n_keys = n_ctx + n_lat
    est = (2 * (n_keys * V_DIM * 2 * 2) + n_keys * V_DIM * 2 + n_keys * 2 * V_DIM * 2
           + 4 * n_lat * V_DIM * 2 + 2 * (2 * rows * n_keys * 4) + 4 * (2 * rows * n_keys * 4) + (8 << 20))
    return pl.pallas_call(
        functools.partial(_attn_kernel, n_ctx=n_ctx, rows=rows),
        grid=(batch, HEADS),
        in_specs=[
            pl.BlockSpec((n_lat, V_DIM), lambda b, h: (b, h)),
            pl.BlockSpec((n_ctx, V_DIM), lambda b, h: (b, h)),
            pl.BlockSpec((n_lat, V_DIM), lambda b, h: (b, h)),
            pl.BlockSpec((n_ctx, V_DIM), lambda b, h: (b, h)),
            pl.BlockSpec((n_lat, V_DIM), lambda b, h: (b, h)),
            pl.BlockSpec((1, LANES), lambda b, h: (0, 0)),
            pl.BlockSpec((1, V_DIM), lambda b, h: (0, 0)),
        ],
        out_specs=pl.BlockSpec((n_lat, V_DIM), lambda b, h: (b, h)),
        out_shape=jax.ShapeDtypeStruct((batch * n_lat, HEADS * V_DIM), BF16),
        scratch_shapes=[
            pltpu.VMEM((n_keys, V_DIM), BF16),
            pltpu.VMEM((n_keys, 2 * V_DIM), BF16),
            pltpu.VMEM((2 * rows, n_keys), F32),
            pltpu.VMEM((2 * rows, n_keys), F32),
        ],
        compiler_params=pltpu.CompilerParams(
            dimension_semantics=("arbitrary", "arbitrary"),
            vmem_limit_bytes=_vmem_limit(est)),
        name="attn",
    )(q, kc, k, vc, v, lam_row, g_subln)


def _scan_kernel(xf_ref, xfp_ref, xfn_ref, xb_ref, xbp_ref, xbn_ref, cw_ref, cb_ref, wg_ref, bg_ref,
                 lam_ref, h0_ref, hf_ref, hb_ref, hfin_ref, xe_ref, wg_scr, a_ref, u_ref, st_ref,
                 *, batch, tc, wc):
    i = pl.program_id(1)
    n = pl.num_programs(1)
    t_grp = LRU_GATE_ROWS // batch
    n_blk, blk = wg_ref.shape[2], wg_ref.shape[3]

    @pl.when(i == 0)
    def _():
        st_ref[...] = h0_ref[...]
        wg_scr[...] = jnp.zeros(wg_scr.shape, BF16)
        for d in range(2):
            for gate in range(2):
                for m in range(n_blk):
                    wg_scr[d, m * blk:(m + 1) * blk, gate * wc + m * blk:gate * wc + (m + 1) * blk] = (
                        wg_ref[d, gate, m].astype(BF16))

    half_taps = [(0.5 * cw_ref[j:j + 1, :]).astype(BF16) for j in range(CONV_WIDTH)]
    half_cb = (0.5 * cb_ref[...]).astype(BF16)

    def coeffs(d, cur_ref, prev_ref, next_ref, chunk):
        prev = prev_ref[...]
        nxt = next_ref[...]
        xe_ref[d, 1:tc + 1] = cur_ref[...]
        xe_ref[d, 0:1] = jnp.where(chunk == 0, jnp.zeros_like(prev), prev)
        xe_ref[d, tc + 1:tc + 3] = jnp.where(chunk == n - 1, jnp.zeros_like(nxt), nxt)
        half_cl = (-0.5 * LRU_C * math.log2(math.e)) * jax.nn.softplus(-lam_ref[d])
        half_bg = 0.5 * bg_ref[d]
        for t0 in range(0, tc, t_grp):
            xh = half_cb
            for j in range(CONV_WIDTH):
                xh = xh + xe_ref[d, t0 + j:t0 + j + t_grp] * half_taps[j]
            xh = xh.reshape(t_grp * batch, wc)
            z = jnp.dot(xh, wg_scr[d], preferred_element_type=F32) + half_bg
            xh = xh.astype(F32)
            a = jnp.exp2(half_cl * jnp.tanh(z[:, 0:wc]) + half_cl)
            y = 1.0 - a * a
            mult = y * lax.rsqrt(jnp.maximum(y, 1e-30))
            u = (mult * xh) * (jnp.tanh(z[:, wc:]) + 1.0)
            a_ref[d, t0:t0 + t_grp] = a.reshape(t_grp, batch, wc)
            u_ref[d, t0:t0 + t_grp] = u.reshape(t_grp, batch, wc)

    coeffs(0, xf_ref, xfp_ref, xfn_ref, i)
    coeffs(1, xb_ref, xbp_ref, xbn_ref, n - 1 - i)

    def scan_step(t, hs):
        tb = tc - 1 - t
        h_f = a_ref[0, t] * hs[0] + u_ref[0, t]
        hf_ref[t] = h_f.astype(hf_ref.dtype)
        h_b = a_ref[1, tb] * hs[1] + u_ref[1, tb]
        hb_ref[tb] = h_b.astype(hb_ref.dtype)
        return h_f, h_b

    h_f, h_b = lax.fori_loop(0, tc, scan_step, (st_ref[0], st_ref[1]), unroll=8)
    st_ref[0] = h_f
    st_ref[1] = h_b

    @pl.when(i == n - 1)
    def _():
        hfin_ref[...] = st_ref[...]


def _scan(x_t, conv_w, conv_b, wg, bg, lam, h0):
    length, batch, width = x_t.shape
    tc, wc = LRU_TC, LRU_WC
    n = length // tc
    blk = wg.shape[-1]
    assert tc % 2 == 0 and LRU_GATE_ROWS % batch == 0 and tc % (LRU_GATE_ROWS // batch) == 0
    assert wc % blk == 0 and wg.shape[2] * blk == width

    def chunk(fwd, i):
        return i if fwd else n - 1 - i

    def cur(fwd):
        return pl.BlockSpec((tc, batch, wc), lambda s, i: (chunk(fwd, i), 0, s))

    def prev(fwd):
        return pl.BlockSpec((1, batch, wc), lambda s, i: (jnp.maximum(chunk(fwd, i) * tc - 1, 0), 0, s))

    def nxt(fwd):
        return pl.BlockSpec((2, batch, wc), lambda s, i: (
            jnp.minimum((chunk(fwd, i) + 1) * (tc // 2), length // 2 - 1), 0, s))

    chunk_f32 = 2 * tc * batch * wc * 4
    est = (2 * (tc + 3) * batch * wc * 4 + 2 * chunk_f32 + 8 * tc * batch * wc * 2
           + 8 * LRU_GATE_ROWS * 2 * wc * 4 + (8 << 20))
    return pl.pallas_call(
        functools.partial(_scan_kernel, batch=batch, tc=tc, wc=wc),
        grid=(width // wc, n),
        in_specs=[
            cur(True), prev(True), nxt(True), cur(False), prev(False), nxt(False),
            pl.BlockSpec((CONV_WIDTH, wc), lambda s, i: (0, s)),
            pl.BlockSpec((1, wc), lambda s, i: (0, s)),
            pl.BlockSpec((2, 2, wc // blk, blk, blk), lambda s, i: (0, 0, s, 0, 0)),
            pl.BlockSpec((2, None, 1, 2 * wc), lambda s, i: (0, s, 0, 0)),
            pl.BlockSpec((2, 1, wc), lambda s, i: (0, 0, s)),
            pl.BlockSpec((2, batch, wc), lambda s, i: (0, 0, s)),
        ],
        out_specs=[cur(True), cur(False), pl.BlockSpec((2, batch, wc), lambda s, i: (0, 0, s))],
        out_shape=[
            jax.ShapeDtypeStruct((length, batch, width), BF16),
            jax.ShapeDtypeStruct((length, batch, width), BF16),
            jax.ShapeDtypeStruct((2, batch, width), F32),
        ],
        scratch_shapes=[
            pltpu.VMEM((2, tc + 3, batch, wc), BF16),
            pltpu.VMEM((2, wc, 2 * wc), BF16),
            pltpu.VMEM((2, tc, batch, wc), F32),
            pltpu.VMEM((2, tc, batch, wc), F32),
            pltpu.VMEM((2, batch, wc), F32),
        ],
        compiler_params=pltpu.CompilerParams(
            dimension_semantics=("arbitrary", "arbitrary"), vmem_limit_bytes=_vmem_limit(est)),
        name="lru",
    )(x_t, x_t, x_t, x_t, x_t, x_t, conv_w, conv_b, wg, bg, lam, h0)


def _merge_kernel(attn_ref, ga_ref, hf_ref, hb_ref, gr_ref, gm_ref, x_ref, mod_ref, gp_ref,
                  wa_ref, wl_ref, wo_ref, o_ref, *, d_model):
    def silu(ref):
        h = ref[...] * 0.5
        return h + h * jnp.tanh(h)

    a_in = attn_ref[...] * silu(ga_ref)
    y_attn = jnp.dot(a_in, wa_ref[...], preferred_element_type=F32)
    l_in = (hf_ref[...] + hb_ref[...]) * silu(gr_ref)
    y_lru = jnp.dot(l_in, wl_ref[...], preferred_element_type=F32)
    t_attn = jnp.tanh(0.5 * gm_ref[:, 0:d_model].astype(F32))
    t_lru = jnp.tanh(0.5 * gm_ref[:, d_model:].astype(F32))
    z = (0.5 * ((t_attn + 1.0) * y_attn + (t_lru + 1.0) * y_lru)).astype(BF16)
    y = jnp.dot(z, wo_ref[...], preferred_element_type=F32)
    ms = jnp.mean(y * y, axis=-1, keepdims=True)
    yn = y * lax.rsqrt(ms + NORM_EPS) * gp_ref[...]
    gate = mod_ref[:, 2 * d_model:]
    o_ref[...] = x_ref[...] + gate * yn


def _merge(attn, ga, hf, hb, gr, gm, x2, mod3, g_post, wa, wl, wo, rows_per_batch):
    n, d = x2.shape
    tm = MERGE_TM
    tiles_per_batch = rows_per_batch // tm
    row = pl.BlockSpec((tm, d), lambda i: (i, 0))
    trow = pl.BlockSpec((tm, d), lambda i: (i % tiles_per_batch, i // tiles_per_batch))
    wspec = pl.BlockSpec((d, d), lambda i: (0, 0), pipeline_mode=pl.Buffered(1))
    est = 3 * d * d * 2 + 2 * (5 * tm * d * 2 + tm * 2 * d * 2 + 2 * tm * d * 4) + 10 * tm * d * 4 + (4 << 20)
    return pl.pallas_call(
        functools.partial(_merge_kernel, d_model=d),
        grid=(n // tm,),
        in_specs=[
            row, row, trow, trow, row,
            pl.BlockSpec((tm, 2 * d), lambda i: (i, 0)),
            row,
            pl.BlockSpec((None, 1, 3 * d), lambda i: (i // tiles_per_batch, 0, 0)),
            pl.BlockSpec((1, d), lambda i: (0, 0)),
            wspec, wspec, wspec,
        ],
        out_specs=row,
        out_shape=jax.ShapeDtypeStruct((n, d), F32),
        compiler_params=pltpu.CompilerParams(
            dimension_semantics=("arbitrary",), vmem_limit_bytes=_vmem_limit(est)),
        name="merge",
    )(attn, ga, hf, hb, gr, gm, x2, mod3, g_post, wa, wl, wo)


def _rope_tables(n_tokens):
    t = np.arange(n_tokens)
    row = (t // GRID_W).astype(np.float64)
    col = (t % GRID_W).astype(np.float64)
    axis_dim = HEAD_DIM // 2
    inv_freq = ROPE_THETA ** (-np.arange(0, axis_dim, 2, dtype=np.float64) / axis_dim)
    d = np.arange(LANES) % HEAD_DIM
    freq = inv_freq[d % (axis_dim // 2)]
    pos = np.where((d < axis_dim)[None, :], row[:, None], col[:, None])
    ang = pos * freq[None, :]
    sign = np.where((d % axis_dim) < axis_dim // 2, -1.0, 1.0)
    cos = np.cos(ang)
    sin = np.sin(ang) * sign[None, :]
    qs = (HEAD_DIM ** -0.5) * math.log2(math.e)
    return tuple(jnp.asarray(a, F32) for a in (cos * qs, sin * qs, cos, sin))


def _gate_params(w_a, w_x, b_a, b_x, wc):
    n_slab = b_a.shape[-1] // wc
    wg = jnp.stack([w_a, w_x], axis=1)
    bg = jnp.concatenate([b_a.reshape(2, n_slab, 1, wc), b_x.reshape(2, n_slab, 1, wc)], axis=-1)
    return wg, bg


def kernel(x, c, ctx, c_ctx, w_mod, b_mod, g_pre, g_post, w_in, lambda_q1, lambda_k1, lambda_q2, lambda_k2,
           g_subln, w_attn_out, conv_w, conv_b, w_rg_a, b_rg_a, w_rg_x, b_rg_x, lru_lambda, w_lru_out, w_out):
    batch, n_lat, d = x.shape
    n_ctx = ctx.shape[1]
    assert w_mod.shape[0] == 1, "single-layer block"
    assert d == HEADS * V_DIM and n_lat % PROJ_TM == 0 and n_lat % MERGE_TM == 0 and n_ctx % BF16_ROWS == 0

    c_all = jnp.zeros((MOD_ROWS, d), F32).at[:batch].set(c).at[batch].set(c_ctx)
    vec = lambda a: a[0].reshape(1, HEAD_DIM)
    mod, lam_row = _modulation(c_all, w_mod[0], b_mod[0].reshape(1, -1),
                               vec(lambda_q1), vec(lambda_k1), vec(lambda_q2), vec(lambda_k2))
    mod3 = mod.reshape(MOD_ROWS, 1, 3 * d)

    w_bf = w_in[0].astype(BF16)
    tables = _rope_tables(n_lat)
    g_pre2 = g_pre[0].reshape(1, d)
    x2 = x.reshape(batch * n_lat, d)
    lat_groups = ((d, "q", 0, False), (d, "k", d, False), (d, None, 2 * d, False), (d, None, 3 * d, False),
                  (d, None, 4 * d, True), (d, None, 5 * d, False), (2 * d, None, 6 * d, False))
    q, k, v, ga, xr, gr, gm = _project(
        x2, mod3, g_pre2, w_bf, tables, lat_groups, n_lat, lambda i, tpb: i // tpb)

    ctx_groups = ((d, None, d, False), (d, None, 2 * d, False), (d, None, 4 * d, True))
    kc, vc, xrc = _project(
        ctx.reshape(batch * n_ctx, d), mod3, g_pre2, w_bf, tables, ctx_groups, n_ctx,
        lambda i, tpb: batch)

    attn = _attention(q, kc, k, vc, v, lam_row, g_subln[0].reshape(1, V_DIM), batch, n_lat, n_ctx)

    wg, bg = _gate_params(w_rg_a[0], w_rg_x[0], b_rg_a[0], b_rg_x[0], LRU_WC)
    lam = lru_lambda[0].reshape(2, 1, d)
    cb = conv_b[0].reshape(1, d)
    zeros = jnp.zeros((2, batch, d), F32)
    _, _, h_ctx = _scan(xrc.reshape(n_ctx, batch, d), conv_w[0], cb, wg, bg, lam, zeros)
    hf, hb, _ = _scan(xr.reshape(n_lat, batch, d), conv_w[0], cb, wg, bg, lam, h_ctx)

    out = _merge(attn, ga, hf.reshape(n_lat, batch * d), hb.reshape(n_lat, batch * d), gr, gm, x2, mod3,
                 g_post[0].reshape(1, d), w_attn_out[0].astype(BF16), w_lru_out[0].astype(BF16),
                 w_out[0].astype(BF16), n_lat)
    return out.reshape(batch, n_lat, d)
```

```python
import functools
import math

import jax
import jax.numpy as jnp
import numpy as np
from jax import lax
from jax.experimental import pallas as pl
from jax.experimental.pallas import tpu as pltpu

F32 = jnp.float32
BF16 = jnp.bfloat16

LANES = 128
SUBLANES = 8
BF16_ROWS = 16
VMEM_BYTES = 64 * 1024 * 1024

GRID_W = 64
HEADS = 8
HEAD_DIM = 64
V_DIM = 2 * HEAD_DIM
CONV_WIDTH = 4
LRU_C = 8.0
ROPE_THETA = 10000.0
NORM_EPS = 1e-6
LAMBDA_INIT = 0.8 - 0.6 * math.exp(-0.3 * 0)

MOD_ROWS = 24
PROJ_TM = 512
MERGE_TM = 512
ATTN_ROWS = 128
ATTN_HEADS = 2
LRU_TC = 256
LRU_WC = 256
LRU_GATE_ROWS = 512


def _sigmoid(x):
    return 0.5 * jnp.tanh(0.5 * x) + 0.5


def _vmem_limit(nbytes):
    return int(min(nbytes, VMEM_BYTES - 8 * 1024 * 1024))


def _mod_kernel(c_ref, w_ref, b_ref, lq1_ref, lk1_ref, lq2_ref, lk2_ref, mod_ref, lam_ref):
    c = c_ref[...]
    s = c * _sigmoid(c)
    mod_ref[...] = jnp.dot(s, w_ref[...], precision=lax.Precision.HIGHEST,
                           preferred_element_type=F32) + b_ref[...]
    s1 = jnp.sum(lq1_ref[...] * lk1_ref[...], axis=-1, keepdims=True)
    s2 = jnp.sum(lq2_ref[...] * lk2_ref[...], axis=-1, keepdims=True)
    lam = jnp.exp(s1) - jnp.exp(s2) + LAMBDA_INIT
    lam_ref[...] = jnp.broadcast_to(lam, lam_ref.shape)


def _modulation(c_all, w_mod, b_mod, lq1, lk1, lq2, lk2):
    d = c_all.shape[1]
    n_out = w_mod.shape[1]
    tn = n_out // 2
    vec = pl.BlockSpec((1, HEAD_DIM), lambda j: (0, 0))
    return pl.pallas_call(
        _mod_kernel,
        grid=(n_out // tn,),
        in_specs=[
            pl.BlockSpec((MOD_ROWS, d), lambda j: (0, 0)),
            pl.BlockSpec((d, tn), lambda j: (0, j)),
            pl.BlockSpec((1, tn), lambda j: (0, j)),
            vec, vec, vec, vec,
        ],
        out_specs=[
            pl.BlockSpec((MOD_ROWS, tn), lambda j: (0, j)),
            pl.BlockSpec((1, LANES), lambda j: (0, 0)),
        ],
        out_shape=[
            jax.ShapeDtypeStruct((MOD_ROWS, n_out), F32),
            jax.ShapeDtypeStruct((1, LANES), F32),
        ],
        name="mod",
    )(c_all, w_mod, b_mod, lq1, lk1, lq2, lk2)


def _rope_slab(xs, cos, sin, low_half):
    up = pltpu.roll(xs, LANES - 16, 1)
    dn = pltpu.roll(xs, 16, 1)
    return xs * cos + jnp.where(low_half, up, dn) * sin


def _proj_kernel(x_ref, mod_ref, g_ref, qcos_ref, qsin_ref, kcos_ref, ksin_ref, *w_and_out_refs,
                 d_model, groups):
    w_refs, out_refs = w_and_out_refs[:len(groups)], w_and_out_refs[len(groups):]
    x = x_ref[...]
    ms = jnp.mean(x * x, axis=-1, keepdims=True)
    y = x * lax.rsqrt(ms + NORM_EPS) * g_ref[...]
    shift = mod_ref[:, 0:d_model]
    scale = mod_ref[:, d_model:2 * d_model]
    h = (y * (1.0 + scale) + shift).astype(BF16)
    lane = lax.broadcasted_iota(jnp.int32, (x.shape[0], LANES), 1)
    low_half = (lane & 16) == 0
    for (width, rope, _, _), w_ref, o_ref in zip(groups, w_refs, out_refs):
        r = jnp.dot(h, w_ref[...], preferred_element_type=F32)
        if rope is None:
            o_ref[...] = r.astype(o_ref.dtype)
        else:
            cos_ref, sin_ref = (qcos_ref, qsin_ref) if rope == "q" else (kcos_ref, ksin_ref)
            cos = cos_ref[...]
            sin = sin_ref[...]
            for s in range(width // LANES):
                sl = slice(s * LANES, (s + 1) * LANES)
                o_ref[:, sl] = _rope_slab(r[:, sl], cos, sin, low_half).astype(o_ref.dtype)


def _project(x2, mod3, g_pre, w, tables, groups, rows_per_batch, mod_row_of_tile):
    n, d = x2.shape
    tm = min(PROJ_TM, rows_per_batch)
    tiles_per_batch = rows_per_batch // tm
    n_batches = n // rows_per_batch
    tab = pl.BlockSpec((tm, LANES), lambda i: (i % tiles_per_batch, 0))
    out_cols = sum(g[0] for g in groups)

    def out_spec(width, time_major):
        if time_major:
            return pl.BlockSpec((tm, width), lambda i: (i % tiles_per_batch, i // tiles_per_batch))
        return pl.BlockSpec((tm, width), lambda i: (i, 0))

    def out_shape(width, time_major):
        shape = (rows_per_batch, n_batches * width) if time_major else (n, width)
        return jax.ShapeDtypeStruct(shape, BF16)

    est = (d * out_cols * 2 + 2 * tm * d * 4 + 2 * tm * out_cols * 2 + 6 * tm * d * 4
           + 8 * tm * LANES * 4 + (4 << 20))

    def w_cols(width, offset):
        assert offset % width == 0
        return pl.BlockSpec((d, width), lambda i: (0, offset // width), pipeline_mode=pl.Buffered(1))

    return pl.pallas_call(
        functools.partial(_proj_kernel, d_model=d, groups=groups),
        grid=(n // tm,),
        in_specs=[
            pl.BlockSpec((tm, d), lambda i: (i, 0)),
            pl.BlockSpec((None, 1, 3 * d), lambda i: (mod_row_of_tile(i, tiles_per_batch), 0, 0)),
            pl.BlockSpec((1, d), lambda i: (0, 0)),
            tab, tab, tab, tab,
        ] + [w_cols(wd, off) for wd, _, off, _ in groups],
        out_specs=[out_spec(wd, tmaj) for wd, _, _, tmaj in groups],
        out_shape=[out_shape(wd, tmaj) for wd, _, _, tmaj in groups],
        compiler_params=pltpu.CompilerParams(
            dimension_semantics=("arbitrary",), vmem_limit_bytes=_vmem_limit(est)),
        name="proj",
    )(x2, mod3, g_pre, *tables, *([w] * len(groups)))


def _attn_kernel(q_ref, kc_ref, k_ref, vc_ref, v_ref, lam_ref, g_ref, o_ref, kall_ref, vext_ref,
                 s0_ref, s1_ref, *, n_ctx, rows):
    n_sub = q_ref.shape[0] // rows
    n_heads = q_ref.shape[1] // V_DIM
    cols = [slice(h * V_DIM, (h + 1) * V_DIM) for h in range(n_heads)]
    for h in range(n_heads):
        kall_ref[h, 0:n_ctx, :] = kc_ref[:, cols[h]]
        kall_ref[h, n_ctx:, :] = k_ref[:, cols[h]]
        vext_ref[h, 0:n_ctx, 0:V_DIM] = vc_ref[:, cols[h]]
        vext_ref[h, n_ctx:, 0:V_DIM] = v_ref[:, cols[h]]
        vext_ref[h, :, V_DIM:] = jnp.ones((vext_ref.shape[1], V_DIM), BF16)
    lane = lax.broadcasted_iota(jnp.int32, (rows, V_DIM), 1)
    first_half = lane < HEAD_DIM
    blocks = [(h, r) for h in range(n_heads) for r in range(n_sub)]
    s_refs = (s0_ref, s1_ref)

    def scores(idx):
        h, r = blocks[idx]
        q = q_ref[r * rows:(r + 1) * rows, cols[h]]
        zero = jnp.zeros_like(q)
        q2 = jnp.concatenate([jnp.where(first_half, q, zero), jnp.where(first_half, zero, q)], axis=0)
        s_refs[idx % 2][...] = lax.dot_general(q2, kall_ref[h], (((1,), (1,)), ((), ())),
                                              preferred_element_type=F32)

    def attend(idx):
        h, r = blocks[idx]
        s = s_refs[idx % 2][...]
        m = jnp.max(s, axis=-1, keepdims=True)
        e = jnp.exp2(s - m).astype(BF16)
        acc = jnp.dot(e, vext_ref[h], preferred_element_type=F32)
        oj = acc[:, 0:V_DIM] / acc[:, V_DIM:]
        o = oj[0:rows] - lam_ref[...] * oj[rows:]
        ms = jnp.mean(o * o, axis=-1, keepdims=True)
        o = o * lax.rsqrt(ms + NORM_EPS) * g_ref[...] * (1.0 - LAMBDA_INIT)
        o_ref[r * rows:(r + 1) * rows, cols[h]] = o.astype(o_ref.dtype)

    scores(0)
    for idx in range(len(blocks)):
        if idx + 1 < len(blocks):
            scores(idx + 1)
        attend(idx)


def _attention(q, kc, k, vc, v, lam_row, g_subln, batch, n_lat, n_ctx):
    rows = ATTN_ROWS
    nh = ATTN_HEADS
    assert n_lat % rows == 0 and HEADS % nh == 0
    n_keys = n_ctx + n_lat
    est = (nh * (2 * (n_keys * V_DIM * 2 * 2) + n_keys * V_DIM * 2 + n_keys * 2 * V_DIM * 2
                 + 4 * n_lat * V_DIM * 2) + 2 * (2 * rows * n_keys * 4) + 4 * (2 * rows * n_keys * 4) + (8 << 20))

    def heads(n_rows):
        return pl.BlockSpec((n_rows, nh * V_DIM), lambda b, h: (b, h))

    return pl.pallas_call(
        functools.partial(_attn_kernel, n_ctx=n_ctx, rows=rows),
        grid=(batch, HEADS // nh),
        in_specs=[
            heads(n_lat), heads(n_ctx), heads(n_lat), heads(n_ctx), heads(n_lat),
            pl.BlockSpec((1, LANES), lambda b, h: (0, 0)),
            pl.BlockSpec((1, V_DIM), lambda b, h: (0, 0)),
        ],
        out_specs=heads(n_lat),
        out_shape=jax.ShapeDtypeStruct((batch * n_lat, HEADS * V_DIM), BF16),
        scratch_shapes=[
            pltpu.VMEM((nh, n_keys, V_DIM), BF16),
            pltpu.VMEM((nh, n_keys, 2 * V_DIM), BF16),
            pltpu.VMEM((2 * rows, n_keys), F32),
            pltpu.VMEM((2 * rows, n_keys), F32),
        ],
        compiler_params=pltpu.CompilerParams(
            dimension_semantics=("arbitrary", "arbitrary"),
            vmem_limit_bytes=_vmem_limit(est)),
        name="attn",
    )(q, kc, k, vc, v, lam_row, g_subln)


def _scan_kernel(xf_ref, xfp_ref, xfn_ref, xb_ref, xbp_ref, xbn_ref, cw_ref, cb_ref, wg_ref, bg_ref,
                 lam_ref, h0_ref, hf_ref, hb_ref, hfin_ref, xe_ref, wg_scr, a_ref, u_ref, st_ref,
                 *, batch, tc, wc):
    i = pl.program_id(1)
    n = pl.num_programs(1)
    t_grp = LRU_GATE_ROWS // batch
    n_blk, blk = wg_ref.shape[2], wg_ref.shape[3]

    @pl.when(i == 0)
    def _():
        st_ref[...] = h0_ref[...]
        wg_scr[...] = jnp.zeros(wg_scr.shape, BF16)
        for d in range(2):
            for gate in range(2):
                for m in range(n_blk):
                    wg_scr[d, m * blk:(m + 1) * blk, gate * wc + m * blk:gate * wc + (m + 1) * blk] = (
                        wg_ref[d, gate, m].astype(BF16))

    half_taps = [(0.5 * cw_ref[j:j + 1, :]).astype(BF16) for j in range(CONV_WIDTH)]
    half_cb = (0.5 * cb_ref[...]).astype(BF16)

    def coeffs(d, cur_ref, prev_ref, next_ref, chunk):
        prev = prev_ref[...]
        nxt = next_ref[...]
        xe_ref[d, 1:tc + 1] = cur_ref[...]
        xe_ref[d, 0:1] = jnp.where(chunk == 0, jnp.zeros_like(prev), prev)
        xe_ref[d, tc + 1:tc + 3] = jnp.where(chunk == n - 1, jnp.zeros_like(nxt), nxt)
        half_cl = (-0.5 * LRU_C * math.log2(math.e)) * jax.nn.softplus(-lam_ref[d])
        half_bg = 0.5 * bg_ref[d]
        for t0 in range(0, tc, t_grp):
            xh = half_cb
            for j in range(CONV_WIDTH):
                xh = xh + xe_ref[d, t0 + j:t0 + j + t_grp] * half_taps[j]
            xh = xh.reshape(t_grp * batch, wc)
            z = jnp.dot(xh, wg_scr[d], preferred_element_type=F32) + half_bg
            xh = xh.astype(F32)
            a = jnp.exp2(half_cl * jnp.tanh(z[:, 0:wc]) + half_cl)
            y = 1.0 - a * a
            mult = y * lax.rsqrt(jnp.maximum(y, 1e-30))
            u = (mult * xh) * (jnp.tanh(z[:, wc:]) + 1.0)
            a_ref[d, t0:t0 + t_grp] = a.reshape(t_grp, batch, wc)
            u_ref[d, t0:t0 + t_grp] = u.reshape(t_grp, batch, wc)

    coeffs(0, xf_ref, xfp_ref, xfn_ref, i)
    coeffs(1, xb_ref, xbp_ref, xbn_ref, n - 1 - i)

    def scan_step(t, hs):
        tb = tc - 1 - t
        h_f = a_ref[0, t] * hs[0] + u_ref[0, t]
        hf_ref[t] = h_f.astype(hf_ref.dtype)
        h_b = a_ref[1, tb] * hs[1] + u_ref[1, tb]
        hb_ref[tb] = h_b.astype(hb_ref.dtype)
        return h_f, h_b

    h_f, h_b = lax.fori_loop(0, tc, scan_step, (st_ref[0], st_ref[1]), unroll=8)
    st_ref[0] = h_f
    st_ref[1] = h_b

    @pl.when(i == n - 1)
    def _():
        hfin_ref[...] = st_ref[...]


def _scan(x_t, conv_w, conv_b, wg, bg, lam, h0):
    length, batch, width = x_t.shape
    tc, wc = LRU_TC, LRU_WC
    n = length // tc
    blk = wg.shape[-1]
    assert tc % 2 == 0 and LRU_GATE_ROWS % batch == 0 and tc % (LRU_GATE_ROWS // batch) == 0
    assert wc % blk == 0 and wg.shape[2] * blk == width

    def chunk(fwd, i):
        return i if fwd else n - 1 - i

    def cur(fwd):
        return pl.BlockSpec((tc, batch, wc), lambda s, i: (chunk(fwd, i), 0, s))

    def prev(fwd):
        return pl.BlockSpec((1, batch, wc), lambda s, i: (jnp.maximum(chunk(fwd, i) * tc - 1, 0), 0, s))

    def nxt(fwd):
        return pl.BlockSpec((2, batch, wc), lambda s, i: (
            jnp.minimum((chunk(fwd, i) + 1) * (tc // 2), length // 2 - 1), 0, s))

    chunk_f32 = 2 * tc * batch * wc * 4
    est = (2 * (tc + 3) * batch * wc * 4 + 2 * chunk_f32 + 8 * tc * batch * wc * 2
           + 8 * LRU_GATE_ROWS * 2 * wc * 4 + (8 << 20))
    return pl.pallas_call(
        functools.partial(_scan_kernel, batch=batch, tc=tc, wc=wc),
        grid=(width // wc, n),
        in_specs=[
            cur(True), prev(True), nxt(True), cur(False), prev(False), nxt(False),
            pl.BlockSpec((CONV_WIDTH, wc), lambda s, i: (0, s)),
            pl.BlockSpec((1, wc), lambda s, i: (0, s)),
            pl.BlockSpec((2, 2, wc // blk, blk, blk), lambda s, i: (0, 0, s, 0, 0)),
            pl.BlockSpec((2, None, 1, 2 * wc), lambda s, i: (0, s, 0, 0)),
            pl.BlockSpec((2, 1, wc), lambda s, i: (0, 0, s)),
            pl.BlockSpec((2, batch, wc), lambda s, i: (0, 0, s)),
        ],
        out_specs=[cur(True), cur(False), pl.BlockSpec((2, batch, wc), lambda s, i: (0, 0, s))],
        out_shape=[
            jax.ShapeDtypeStruct((length, batch, width), BF16),
            jax.ShapeDtypeStruct((length, batch, width), BF16),
            jax.ShapeDtypeStruct((2, batch, width), F32),
        ],
        scratch_shapes=[
            pltpu.VMEM((2, tc + 3, batch, wc), BF16),
            pltpu.VMEM((2, wc, 2 * wc), BF16),
            pltpu.VMEM((2, tc, batch, wc), F32),
            pltpu.VMEM((2, tc, batch, wc), F32),
            pltpu.VMEM((2, batch, wc), F32),
        ],
        compiler_params=pltpu.CompilerParams(
            dimension_semantics=("arbitrary", "arbitrary"), vmem_limit_bytes=_vmem_limit(est)),
        name="lru",
    )(x_t, x_t, x_t, x_t, x_t, x_t, conv_w, conv_b, wg, bg, lam, h0)


def _merge_kernel(attn_ref, ga_ref, hf_ref, hb_ref, gr_ref, gm_ref, x_ref, mod_ref, gp_ref,
                  wa_ref, wl_ref, wo_ref, o_ref, *, d_model):
    def silu(ref):
        h = ref[...] * 0.5
        return h + h * jnp.tanh(h)

    a_in = attn_ref[...] * silu(ga_ref)
    y_attn = jnp.dot(a_in, wa_ref[...], preferred_element_type=F32)
    l_in = (hf_ref[...] + hb_ref[...]) * silu(gr_ref)
    y_lru = jnp.dot(l_in, wl_ref[...], preferred_element_type=F32)
    t_attn = jnp.tanh(0.5 * gm_ref[:, 0:d_model].astype(F32))
    t_lru = jnp.tanh(0.5 * gm_ref[:, d_model:].astype(F32))
    z = (0.5 * ((t_attn + 1.0) * y_attn + (t_lru + 1.0) * y_lru)).astype(BF16)
    y = jnp.dot(z, wo_ref[...], preferred_element_type=F32)
    ms = jnp.mean(y * y, axis=-1, keepdims=True)
    yn = y * lax.rsqrt(ms + NORM_EPS) * gp_ref[...]
    gate = mod_ref[:, 2 * d_model:]
    o_ref[...] = x_ref[...] + gate * yn


def _merge(attn, ga, hf, hb, gr, gm, x2, mod3, g_post, wa, wl, wo, rows_per_batch):
    n, d = x2.shape
    tm = MERGE_TM
    tiles_per_batch = rows_per_batch // tm
    row = pl.BlockSpec((tm, d), lambda i: (i, 0))
    trow = pl.BlockSpec((tm, d), lambda i: (i % tiles_per_batch, i // tiles_per_batch))
    wspec = pl.BlockSpec((d, d), lambda i: (0, 0), pipeline_mode=pl.Buffered(1))
    est = 3 * d * d * 2 + 2 * (5 * tm * d * 2 + tm * 2 * d * 2 + 2 * tm * d * 4) + 10 * tm * d * 4 + (4 << 20)
    return pl.pallas_call(
        functools.partial(_merge_kernel, d_model=d),
        grid=(n // tm,),
        in_specs=[
            row, row, trow, trow, row,
            pl.BlockSpec((tm, 2 * d), lambda i: (i, 0)),
            row,
            pl.BlockSpec((None, 1, 3 * d), lambda i: (i // tiles_per_batch, 0, 0)),
            pl.BlockSpec((1, d), lambda i: (0, 0)),
            wspec, wspec, wspec,
        ],
        out_specs=row,
        out_shape=jax.ShapeDtypeStruct((n, d), F32),
        compiler_params=pltpu.CompilerParams(
            dimension_semantics=("arbitrary",), vmem_limit_bytes=_vmem_limit(est)),
        name="merge",
    )(attn, ga, hf, hb, gr, gm, x2, mod3, g_post, wa, wl, wo)


def _rope_tables(n_tokens):
    t = np.arange(n_tokens)
    row = (t // GRID_W).astype(np.float64)
    col = (t % GRID_W).astype(np.float64)
    axis_dim = HEAD_DIM // 2
    inv_freq = ROPE_THETA ** (-np.arange(0, axis_dim, 2, dtype=np.float64) / axis_dim)
    d = np.arange(LANES) % HEAD_DIM
    freq = inv_freq[d % (axis_dim // 2)]
    pos = np.where((d < axis_dim)[None, :], row[:, None], col[:, None])
    ang = pos * freq[None, :]
    sign = np.where((d % axis_dim) < axis_dim // 2, -1.0, 1.0)
    cos = np.cos(ang)
    sin = np.sin(ang) * sign[None, :]
    qs = (HEAD_DIM ** -0.5) * math.log2(math.e)
    return tuple(jnp.asarray(a, F32) for a in (cos * qs, sin * qs, cos, sin))


def _gate_params(w_a, w_x, b_a, b_x, wc):
    n_slab = b_a.shape[-1] // wc
    wg = jnp.stack([w_a, w_x], axis=1)
    bg = jnp.concatenate([b_a.reshape(2, n_slab, 1, wc), b_x.reshape(2, n_slab, 1, wc)], axis=-1)
    return wg, bg


def kernel(x, c, ctx, c_ctx, w_mod, b_mod, g_pre, g_post, w_in, lambda_q1, lambda_k1, lambda_q2, lambda_k2,
           g_subln, w_attn_out, conv_w, conv_b, w_rg_a, b_rg_a, w_rg_x, b_rg_x, lru_lambda, w_lru_out, w_out):
    batch, n_lat, d = x.shape
    n_ctx = ctx.shape[1]
    assert w_mod.shape[0] == 1, "single-layer block"
    assert d == HEADS * V_DIM and n_lat % PROJ_TM == 0 and n_lat % MERGE_TM == 0 and n_ctx % BF16_ROWS == 0

    c_all = jnp.zeros((MOD_ROWS, d), F32).at[:batch].set(c).at[batch].set(c_ctx)
    vec = lambda a: a[0].reshape(1, HEAD_DIM)
    mod, lam_row = _modulation(c_all, w_mod[0], b_mod[0].reshape(1, -1),
                               vec(lambda_q1), vec(lambda_k1), vec(lambda_q2), vec(lambda_k2))
    mod3 = mod.reshape(MOD_ROWS, 1, 3 * d)

    w_bf = w_in[0].astype(BF16)
    tables = _rope_tables(n_lat)
    g_pre2 = g_pre[0].reshape(1, d)
    x2 = x.reshape(batch * n_lat, d)
    lat_groups = ((d, "q", 0, False), (d, "k", d, False), (d, None, 2 * d, False), (d, None, 3 * d, False),
                  (d, None, 4 * d, True), (d, None, 5 * d, False), (2 * d, None, 6 * d, False))
    q, k, v, ga, xr, gr, gm = _project(
        x2, mod3, g_pre2, w_bf, tables, lat_groups, n_lat, lambda i, tpb: i // tpb)

    ctx_groups = ((d, None, d, False), (d, None, 2 * d, False), (d, None, 4 * d, True))
    kc, vc, xrc = _project(
        ctx.reshape(batch * n_ctx, d), mod3, g_pre2, w_bf, tables, ctx_groups, n_ctx,
        lambda i, tpb: batch)

    attn = _attention(q, kc, k, vc, v, lam_row, g_subln[0].reshape(1, V_DIM), batch, n_lat, n_ctx)

    wg, bg = _gate_params(w_rg_a[0], w_rg_x[0], b_rg_a[0], b_rg_x[0], LRU_WC)
    lam = lru_lambda[0].reshape(2, 1, d)
    cb = conv_b[0].reshape(1, d)
    zeros = jnp.zeros((2, batch, d), F32)
    _, _, h_ctx = _scan(xrc.reshape(n_ctx, batch, d), conv_w[0], cb, wg, bg, lam, zeros)
    hf, hb, _ = _scan(xr.reshape(n_lat, batch, d), conv_w[0], cb, wg, bg, lam, h_ctx)

    out = _merge(attn, ga, hf.reshape(n_lat, batch * d), hb.reshape(n_lat, batch * d), gr, gm, x2, mod3,
                 g_post[0].reshape(1, d), w_attn_out[0].astype(BF16), w_lru_out[0].astype(BF16),
                 w_out[0].astype(BF16), n_lat)
    return out.reshape(batch, n_lat, d)
```

```python
import functools
import math

import jax
import jax.numpy as jnp
import numpy as np
from jax import lax
from jax.experimental import pallas as pl
from jax.experimental.pallas import tpu as pltpu

F32 = jnp.float32
BF16 = jnp.bfloat16

LANES = 128
SUBLANES = 8
BF16_ROWS = 16
VMEM_BYTES = 64 * 1024 * 1024

GRID_W = 64
HEADS = 8
HEAD_DIM = 64
V_DIM = 2 * HEAD_DIM
CONV_WIDTH = 4
LRU_C = 8.0
ROPE_THETA = 10000.0
NORM_EPS = 1e-6
LAMBDA_INIT = 0.8 - 0.6 * math.exp(-0.3 * 0)

MOD_ROWS = 24
PROJ_TM = 512
MERGE_TM = 512
ATTN_ROWS = 128
ATTN_HEADS = 4
LRU_TC = 256
LRU_WC = 256
LRU_GATE_ROWS = 512


def _sigmoid(x):
    return 0.5 * jnp.tanh(0.5 * x) + 0.5


def _vmem_limit(nbytes):
    return int(min(nbytes, VMEM_BYTES - 8 * 1024 * 1024))


def _mod_kernel(c_ref, w_ref, b_ref, lq1_ref, lk1_ref, lq2_ref, lk2_ref, mod_ref, lam_ref):
    c = c_ref[...]
    s = c * _sigmoid(c)
    mod_ref[...] = jnp.dot(s, w_ref[...], precision=lax.Precision.HIGHEST,
                           preferred_element_type=F32) + b_ref[...]
    s1 = jnp.sum(lq1_ref[...] * lk1_ref[...], axis=-1, keepdims=True)
    s2 = jnp.sum(lq2_ref[...] * lk2_ref[...], axis=-1, keepdims=True)
    lam = jnp.exp(s1) - jnp.exp(s2) + LAMBDA_INIT
    lam_ref[...] = jnp.broadcast_to(lam, lam_ref.shape)


def _modulation(c_all, w_mod, b_mod, lq1, lk1, lq2, lk2):
    d = c_all.shape[1]
    n_out = w_mod.shape[1]
    tn = n_out // 2
    vec = pl.BlockSpec((1, HEAD_DIM), lambda j: (0, 0))
    return pl.pallas_call(
        _mod_kernel,
        grid=(n_out // tn,),
        in_specs=[
            pl.BlockSpec((MOD_ROWS, d), lambda j: (0, 0)),
            pl.BlockSpec((d, tn), lambda j: (0, j)),
            pl.BlockSpec((1, tn), lambda j: (0, j)),
            vec, vec, vec, vec,
        ],
        out_specs=[
            pl.BlockSpec((MOD_ROWS, tn), lambda j: (0, j)),
            pl.BlockSpec((1, LANES), lambda j: (0, 0)),
        ],
        out_shape=[
            jax.ShapeDtypeStruct((MOD_ROWS, n_out), F32),
            jax.ShapeDtypeStruct((1, LANES), F32),
        ],
        name="mod",
    )(c_all, w_mod, b_mod, lq1, lk1, lq2, lk2)


def _rope_slab(xs, cos, sin, low_half):
    up = pltpu.roll(xs, LANES - 16, 1)
    dn = pltpu.roll(xs, 16, 1)
    return xs * cos + jnp.where(low_half, up, dn) * sin


def _proj_kernel(x_ref, mod_ref, g_ref, qcos_ref, qsin_ref, kcos_ref, ksin_ref, *w_and_out_refs,
                 d_model, groups):
    w_refs, out_refs = w_and_out_refs[:len(groups)], w_and_out_refs[len(groups):]
    x = x_ref[...]
    ms = jnp.mean(x * x, axis=-1, keepdims=True)
    y = x * lax.rsqrt(ms + NORM_EPS) * g_ref[...]
    shift = mod_ref[:, 0:d_model]
    scale = mod_ref[:, d_model:2 * d_model]
    h = (y * (1.0 + scale) + shift).astype(BF16)
    lane = lax.broadcasted_iota(jnp.int32, (x.shape[0], LANES), 1)
    low_half = (lane & 16) == 0
    for (width, rope, _, _), w_ref, o_ref in zip(groups, w_refs, out_refs):
        r = jnp.dot(h, w_ref[...], preferred_element_type=F32)
        if rope is None:
            o_ref[...] = r.astype(o_ref.dtype)
        else:
            cos_ref, sin_ref = (qcos_ref, qsin_ref) if rope == "q" else (kcos_ref, ksin_ref)
            cos = cos_ref[...]
            sin = sin_ref[...]
            for s in range(width // LANES):
                sl = slice(s * LANES, (s + 1) * LANES)
                o_ref[:, sl] = _rope_slab(r[:, sl], cos, sin, low_half).astype(o_ref.dtype)


def _project(x2, mod3, g_pre, w, tables, groups, rows_per_batch, mod_row_of_tile):
    n, d = x2.shape
    tm = min(PROJ_TM, rows_per_batch)
    tiles_per_batch = rows_per_batch // tm
    n_batches = n // rows_per_batch
    tab = pl.BlockSpec((tm, LANES), lambda i: (i % tiles_per_batch, 0))
    out_cols = sum(g[0] for g in groups)

    def out_spec(width, time_major):
        if time_major:
            return pl.BlockSpec((tm, width), lambda i: (i % tiles_per_batch, i // tiles_per_batch))
        return pl.BlockSpec((tm, width), lambda i: (i, 0))

    def out_shape(width, time_major):
        shape = (rows_per_batch, n_batches * width) if time_major else (n, width)
        return jax.ShapeDtypeStruct(shape, BF16)

    est = (d * out_cols * 2 + 2 * tm * d * 4 + 2 * tm * out_cols * 2 + 6 * tm * d * 4
           + 8 * tm * LANES * 4 + (4 << 20))

    def w_cols(width, offset):
        assert offset % width == 0
        return pl.BlockSpec((d, width), lambda i: (0, offset // width), pipeline_mode=pl.Buffered(1))

    return pl.pallas_call(
        functools.partial(_proj_kernel, d_model=d, groups=groups),
        grid=(n // tm,),
        in_specs=[
            pl.BlockSpec((tm, d), lambda i: (i, 0)),
            pl.BlockSpec((None, 1, 3 * d), lambda i: (mod_row_of_tile(i, tiles_per_batch), 0, 0)),
            pl.BlockSpec((1, d), lambda i: (0, 0)),
            tab, tab, tab, tab,
        ] + [w_cols(wd, off) for wd, _, off, _ in groups],
        out_specs=[out_spec(wd, tmaj) for wd, _, _, tmaj in groups],
        out_shape=[out_shape(wd, tmaj) for wd, _, _, tmaj in groups],
        compiler_params=pltpu.CompilerParams(
            dimension_semantics=("arbitrary",), vmem_limit_bytes=_vmem_limit(est)),
        name="proj",
    )(x2, mod3, g_pre, *tables, *([w] * len(groups)))


def _attn_kernel(q_ref, kc_ref, k_ref, vc_ref, v_ref, lam_ref, g_ref, o_ref, kall_ref, vext_ref,
                 s0_ref, s1_ref, *, n_ctx, rows):
    n_sub = q_ref.shape[0] // rows
    n_heads = q_ref.shape[1] // V_DIM
    cols = [slice(h * V_DIM, (h + 1) * V_DIM) for h in range(n_heads)]
    for h in range(n_heads):
        kall_ref[h, 0:n_ctx, :] = kc_ref[:, cols[h]]
        kall_ref[h, n_ctx:, :] = k_ref[:, cols[h]]
        vext_ref[h, 0:n_ctx, 0:V_DIM] = vc_ref[:, cols[h]]
        vext_ref[h, n_ctx:, 0:V_DIM] = v_ref[:, cols[h]]
        vext_ref[h, :, V_DIM:] = jnp.ones((vext_ref.shape[1], V_DIM), BF16)
    lane = lax.broadcasted_iota(jnp.int32, (rows, V_DIM), 1)
    first_half = lane < HEAD_DIM
    blocks = [(h, r) for h in range(n_heads) for r in range(n_sub)]
    s_refs = (s0_ref, s1_ref)

    def scores(idx):
        h, r = blocks[idx]
        q = q_ref[r * rows:(r + 1) * rows, cols[h]]
        zero = jnp.zeros_like(q)
        q2 = jnp.concatenate([jnp.where(first_half, q, zero), jnp.where(first_half, zero, q)], axis=0)
        s_refs[idx % 2][...] = lax.dot_general(q2, kall_ref[h], (((1,), (1,)), ((), ())),
                                              preferred_element_type=F32)

    def attend(idx):
        h, r = blocks[idx]
        s = s_refs[idx % 2][...]
        m = jnp.max(s, axis=-1, keepdims=True)
        e = jnp.exp2(s - m).astype(BF16)
        acc = jnp.dot(e, vext_ref[h], preferred_element_type=F32)
        oj = acc[:, 0:V_DIM] / acc[:, V_DIM:]
        o = oj[0:rows] - lam_ref[...] * oj[rows:]
        ms = jnp.mean(o * o, axis=-1, keepdims=True)
        o = o * lax.rsqrt(ms + NORM_EPS) * g_ref[...] * (1.0 - LAMBDA_INIT)
        o_ref[r * rows:(r + 1) * rows, cols[h]] = o.astype(o_ref.dtype)

    scores(0)
    for idx in range(len(blocks)):
        if idx + 1 < len(blocks):
            scores(idx + 1)
        attend(idx)


def _attention(q, kc, k, vc, v, lam_row, g_subln, batch, n_lat, n_ctx):
    rows = ATTN_ROWS
    nh = ATTN_HEADS
    assert n_lat % rows == 0 and HEADS % nh == 0
    n_keys = n_ctx + n_lat
    est = (nh * (2 * (n_keys * V_DIM * 2 * 2) + n_keys * V_DIM * 2 + n_keys * 2 * V_DIM * 2
                 + 4 * n_lat * V_DIM * 2) + 2 * (2 * rows * n_keys * 4) + 4 * (2 * rows * n_keys * 4) + (8 << 20))

    def heads(n_rows):
        return pl.BlockSpec((n_rows, nh * V_DIM), lambda b, h: (b, h))

    return pl.pallas_call(
        functools.partial(_attn_kernel, n_ctx=n_ctx, rows=rows),
        grid=(batch, HEADS // nh),
        in_specs=[
            heads(n_lat), heads(n_ctx), heads(n_lat), heads(n_ctx), heads(n_lat),
            pl.BlockSpec((1, LANES), lambda b, h: (0, 0)),
            pl.BlockSpec((1, V_DIM), lambda b, h: (0, 0)),
        ],
        out_specs=heads(n_lat),
        out_shape=jax.ShapeDtypeStruct((batch * n_lat, HEADS * V_DIM), BF16),
        scratch_shapes=[
            pltpu.VMEM((nh, n_keys, V_DIM), BF16),
            pltpu.VMEM((nh, n_keys, 2 * V_DIM), BF16),
            pltpu.VMEM((2 * rows, n_keys), F32),
            pltpu.VMEM((2 * rows, n_keys), F32),
        ],
        compiler_params=pltpu.CompilerParams(
            dimension_semantics=("arbitrary", "arbitrary"),
            vmem_limit_bytes=_vmem_limit(est)),
        name="attn",
    )(q, kc, k, vc, v, lam_row, g_subln)


def _scan_kernel(xf_ref, xfp_ref, xfn_ref, xb_ref, xbp_ref, xbn_ref, cw_ref, cb_ref, wg_ref, bg_ref,
                 lam_ref, h0_ref, hf_ref, hb_ref, hfin_ref, xe_ref, wg_scr, a_ref, u_ref, st_ref,
                 *, batch, tc, wc):
    i = pl.program_id(1)
    n = pl.num_programs(1)
    t_grp = LRU_GATE_ROWS // batch
    n_blk, blk = wg_ref.shape[2], wg_ref.shape[3]

    @pl.when(i == 0)
    def _():
        st_ref[...] = h0_ref[...]
        wg_scr[...] = jnp.zeros(wg_scr.shape, BF16)
        for d in range(2):
            for gate in range(2):
                for m in range(n_blk):
                    wg_scr[d, m * blk:(m + 1) * blk, gate * wc + m * blk:gate * wc + (m + 1) * blk] = (
                        wg_ref[d, gate, m].astype(BF16))

    half_taps = [(0.5 * cw_ref[j:j + 1, :]).astype(BF16) for j in range(CONV_WIDTH)]
    half_cb = (0.5 * cb_ref[...]).astype(BF16)

    def coeffs(d, cur_ref, prev_ref, next_ref, chunk):
        prev = prev_ref[...]
        nxt = next_ref[...]
        xe_ref[d, 1:tc + 1] = cur_ref[...]
        xe_ref[d, 0:1] = jnp.where(chunk == 0, jnp.zeros_like(prev), prev)
        xe_ref[d, tc + 1:tc + 3] = jnp.where(chunk == n - 1, jnp.zeros_like(nxt), nxt)
        half_cl = (-0.5 * LRU_C * math.log2(math.e)) * jax.nn.softplus(-lam_ref[d])
        half_bg = 0.5 * bg_ref[d]
        for t0 in range(0, tc, t_grp):
            xh = half_cb
            for j in range(CONV_WIDTH):
                xh = xh + xe_ref[d, t0 + j:t0 + j + t_grp] * half_taps[j]
            xh = xh.reshape(t_grp * batch, wc)
            z = jnp.dot(xh, wg_scr[d], preferred_element_type=F32) + half_bg
            xh = xh.astype(F32)
            a = jnp.exp2(half_cl * jnp.tanh(z[:, 0:wc]) + half_cl)
            y = 1.0 - a * a
            mult = y * lax.rsqrt(jnp.maximum(y, 1e-30))
            u = (mult * xh) * (jnp.tanh(z[:, wc:]) + 1.0)
            a_ref[d, t0:t0 + t_grp] = a.reshape(t_grp, batch, wc)
            u_ref[d, t0:t0 + t_grp] = u.reshape(t_grp, batch, wc)

    coeffs(0, xf_ref, xfp_ref, xfn_ref, i)
    coeffs(1, xb_ref, xbp_ref, xbn_ref, n - 1 - i)

    def scan_step(t, hs):
        tb = tc - 1 - t
        h_f = a_ref[0, t] * hs[0] + u_ref[0, t]
        hf_ref[t] = h_f.astype(hf_ref.dtype)
        h_b = a_ref[1, tb] * hs[1] + u_ref[1, tb]
        hb_ref[tb] = h_b.astype(hb_ref.dtype)
        return h_f, h_b

    h_f, h_b = lax.fori_loop(0, tc, scan_step, (st_ref[0], st_ref[1]), unroll=8)
    st_ref[0] = h_f
    st_ref[1] = h_b

    @pl.when(i == n - 1)
    def _():
        hfin_ref[...] = st_ref[...]


def _scan(x_t, conv_w, conv_b, wg, bg, lam, h0):
    length, batch, width = x_t.shape
    tc, wc = LRU_TC, LRU_WC
    n = length // tc
    blk = wg.shape[-1]
    assert tc % 2 == 0 and LRU_GATE_ROWS % batch == 0 and tc % (LRU_GATE_ROWS // batch) == 0
    assert wc % blk == 0 and wg.shape[2] * blk == width

    def chunk(fwd, i):
        return i if fwd else n - 1 - i

    def cur(fwd):
        return pl.BlockSpec((tc, batch, wc), lambda s, i: (chunk(fwd, i), 0, s))

    def prev(fwd):
        return pl.BlockSpec((1, batch, wc), lambda s, i: (jnp.maximum(chunk(fwd, i) * tc - 1, 0), 0, s))

    def nxt(fwd):
        return pl.BlockSpec((2, batch, wc), lambda s, i: (
            jnp.minimum((chunk(fwd, i) + 1) * (tc // 2), length // 2 - 1), 0, s))

    chunk_f32 = 2 * tc * batch * wc * 4
    est = (2 * (tc + 3) * batch * wc * 4 + 2 * chunk_f32 + 8 * tc * batch * wc * 2
           + 8 * LRU_GATE_ROWS * 2 * wc * 4 + (8 << 20))
    return pl.pallas_call(
        functools.partial(_scan_kernel, batch=batch, tc=tc, wc=wc),
        grid=(width // wc, n),
        in_specs=[
            cur(True), prev(True), nxt(True), cur(False), prev(False), nxt(False),
            pl.BlockSpec((CONV_WIDTH, wc), lambda s, i: (0, s)),
            pl.BlockSpec((1, wc), lambda s, i: (0, s)),
            pl.BlockSpec((2, 2, wc // blk, blk, blk), lambda s, i: (0, 0, s, 0, 0)),
            pl.BlockSpec((2, None, 1, 2 * wc), lambda s, i: (0, s, 0, 0)),
            pl.BlockSpec((2, 1, wc), lambda s, i: (0, 0, s)),
            pl.BlockSpec((2, batch, wc), lambda s, i: (0, 0, s)),
        ],
        out_specs=[cur(True), cur(False), pl.BlockSpec((2, batch, wc), lambda s, i: (0, 0, s))],
        out_shape=[
            jax.ShapeDtypeStruct((length, batch, width), BF16),
            jax.ShapeDtypeStruct((length, batch, width), BF16),
            jax.ShapeDtypeStruct((2, batch, width), F32),
        ],
        scratch_shapes=[
            pltpu.VMEM((2, tc + 3, batch, wc), BF16),
            pltpu.VMEM((2, wc, 2 * wc), BF16),
            pltpu.VMEM((2, tc, batch, wc), F32),
            pltpu.VMEM((2, tc, batch, wc), F32),
            pltpu.VMEM((2, batch, wc), F32),
        ],
        compiler_params=pltpu.CompilerParams(
            dimension_semantics=("arbitrary", "arbitrary"), vmem_limit_bytes=_vmem_limit(est)),
        name="lru",
    )(x_t, x_t, x_t, x_t, x_t, x_t, conv_w, conv_b, wg, bg, lam, h0)


def _merge_kernel(attn_ref, ga_ref, hf_ref, hb_ref, gr_ref, gm_ref, x_ref, mod_ref, gp_ref,
                  wa_ref, wl_ref, wo_ref, o_ref, *, d_model):
    def silu(ref):
        h = ref[...] * 0.5
        return h + h * jnp.tanh(h)

    a_in = attn_ref[...] * silu(ga_ref)
    y_attn = jnp.dot(a_in, wa_ref[...], preferred_element_type=F32)
    l_in = (hf_ref[...] + hb_ref[...]) * silu(gr_ref)
    y_lru = jnp.dot(l_in, wl_ref[...], preferred_element_type=F32)
    t_attn = jnp.tanh(0.5 * gm_ref[:, 0:d_model].astype(F32))
    t_lru = jnp.tanh(0.5 * gm_ref[:, d_model:].astype(F32))
    z = (0.5 * ((t_attn + 1.0) * y_attn + (t_lru + 1.0) * y_lru)).astype(BF16)
    y = jnp.dot(z, wo_ref[...], preferred_element_type=F32)
    ms = jnp.mean(y * y, axis=-1, keepdims=True)
    yn = y * lax.rsqrt(ms + NORM_EPS) * gp_ref[...]
    gate = mod_ref[:, 2 * d_model:]
    o_ref[...] = x_ref[...] + gate * yn


def _merge(attn, ga, hf, hb, gr, gm, x2, mod3, g_post, wa, wl, wo, rows_per_batch):
    n, d = x2.shape
    tm = MERGE_TM
    tiles_per_batch = rows_per_batch // tm
    row = pl.BlockSpec((tm, d), lambda i: (i, 0))
    trow = pl.BlockSpec((tm, d), lambda i: (i % tiles_per_batch, i // tiles_per_batch))
    wspec = pl.BlockSpec((d, d), lambda i: (0, 0), pipeline_mode=pl.Buffered(1))
    est = 3 * d * d * 2 + 2 * (5 * tm * d * 2 + tm * 2 * d * 2 + 2 * tm * d * 4) + 10 * tm * d * 4 + (4 << 20)
    return pl.pallas_call(
        functools.partial(_merge_kernel, d_model=d),
        grid=(n // tm,),
        in_specs=[
            row, row, trow, trow, row,
            pl.BlockSpec((tm, 2 * d), lambda i: (i, 0)),
            row,
            pl.BlockSpec((None, 1, 3 * d), lambda i: (i // tiles_per_batch, 0, 0)),
            pl.BlockSpec((1, d), lambda i: (0, 0)),
            wspec, wspec, wspec,
        ],
        out_specs=row,
        out_shape=jax.ShapeDtypeStruct((n, d), F32),
        compiler_params=pltpu.CompilerParams(
            dimension_semantics=("arbitrary",), vmem_limit_bytes=_vmem_limit(est)),
        name="merge",
    )(attn, ga, hf, hb, gr, gm, x2, mod3, g_post, wa, wl, wo)


def _rope_tables(n_tokens):
    t = np.arange(n_tokens)
    row = (t // GRID_W).astype(np.float64)
    col = (t % GRID_W).astype(np.float64)
    axis_dim = HEAD_DIM // 2
    inv_freq = ROPE_THETA ** (-np.arange(0, axis_dim, 2, dtype=np.float64) / axis_dim)
    d = np.arange(LANES) % HEAD_DIM
    freq = inv_freq[d % (axis_dim // 2)]
    pos = np.where((d < axis_dim)[None, :], row[:, None], col[:, None])
    ang = pos * freq[None, :]
    sign = np.where((d % axis_dim) < axis_dim // 2, -1.0, 1.0)
    cos = np.cos(ang)
    sin = np.sin(ang) * sign[None, :]
    qs = (HEAD_DIM ** -0.5) * math.log2(math.e)
    return tuple(jnp.asarray(a, F32) for a in (cos * qs, sin * qs, cos, sin))


def _gate_params(w_a, w_x, b_a, b_x, wc):
    n_slab = b_a.shape[-1] // wc
    wg = jnp.stack([w_a, w_x], axis=1)
    bg = jnp.concatenate([b_a.reshape(2, n_slab, 1, wc), b_x.reshape(2, n_slab, 1, wc)], axis=-1)
    return wg, bg


def kernel(x, c, ctx, c_ctx, w_mod, b_mod, g_pre, g_post, w_in, lambda_q1, lambda_k1, lambda_q2, lambda_k2,
           g_subln, w_attn_out, conv_w, conv_b, w_rg_a, b_rg_a, w_rg_x, b_rg_x, lru_lambda, w_lru_out, w_out):
    batch, n_lat, d = x.shape
    n_ctx = ctx.shape[1]
    assert w_mod.shape[0] == 1, "single-layer block"
    assert d == HEADS * V_DIM and n_lat % PROJ_TM == 0 and n_lat % MERGE_TM == 0 and n_ctx % BF16_ROWS == 0

    c_all = jnp.zeros((MOD_ROWS, d), F32).at[:batch].set(c).at[batch].set(c_ctx)
    vec = lambda a: a[0].reshape(1, HEAD_DIM)
    mod, lam_row = _modulation(c_all, w_mod[0], b_mod[0].reshape(1, -1),
                               vec(lambda_q1), vec(lambda_k1), vec(lambda_q2), vec(lambda_k2))
    mod3 = mod.reshape(MOD_ROWS, 1, 3 * d)

    w_bf = w_in[0].astype(BF16)
    tables = _rope_tables(n_lat)
    g_pre2 = g_pre[0].reshape(1, d)
    x2 = x.reshape(batch * n_lat, d)
    lat_groups = ((d, "q", 0, False), (d, "k", d, False), (d, None, 2 * d, False), (d, None, 3 * d, False),
                  (d, None, 4 * d, True), (d, None, 5 * d, False), (2 * d, None, 6 * d, False))
    q, k, v, ga, xr, gr, gm = _project(
        x2, mod3, g_pre2, w_bf, tables, lat_groups, n_lat, lambda i, tpb: i // tpb)

    ctx_groups = ((d, None, d, False), (d, None, 2 * d, False), (d, None, 4 * d, True))
    kc, vc, xrc = _project(
        ctx.reshape(batch * n_ctx, d), mod3, g_pre2, w_bf, tables, ctx_groups, n_ctx,
        lambda i, tpb: batch)

    attn = _attention(q, kc, k, vc, v, lam_row, g_subln[0].reshape(1, V_DIM), batch, n_lat, n_ctx)

    wg, bg = _gate_params(w_rg_a[0], w_rg_x[0], b_rg_a[0], b_rg_x[0], LRU_WC)
    lam = lru_lambda[0].reshape(2, 1, d)
    cb = conv_b[0].reshape(1, d)
    zeros = jnp.zeros((2, batch, d), F32)
    _, _, h_ctx = _scan(xrc.reshape(n_ctx, batch, d), conv_w[0], cb, wg, bg, lam, zeros)
    hf, hb, _ = _scan(xr.reshape(n_lat, batch, d), conv_w[0], cb, wg, bg, lam, h_ctx)

    out = _merge(attn, ga, hf.reshape(n_lat, batch * d), hb.reshape(n_lat, batch * d), gr, gm, x2, mod3,
                 g_post[0].reshape(1, d), w_attn_out[0].astype(BF16), w_lru_out[0].astype(BF16),
                 w_out[0].astype(BF16), n_lat)
    return out.reshape(batch, n_lat, d)
```

```python
import functools
import math

import jax
import jax.numpy as jnp
import numpy as np
from jax import lax
from jax.experimental import pallas as pl
from jax.experimental.pallas import tpu as pltpu

F32 = jnp.float32
BF16 = jnp.bfloat16

LANES = 128
SUBLANES = 8
BF16_ROWS = 16
VMEM_BYTES = 64 * 1024 * 1024

GRID_W = 64
HEADS = 8
HEAD_DIM = 64
V_DIM = 2 * HEAD_DIM
CONV_WIDTH = 4
LRU_C = 8.0
ROPE_THETA = 10000.0
NORM_EPS = 1e-6
LAMBDA_INIT = 0.8 - 0.6 * math.exp(-0.3 * 0)

MOD_ROWS = 24
PROJ_TM = 512
MERGE_TM = 512
ATTN_ROWS = 128
ATTN_HEADS = 2
LRU_TC = 256
LRU_WC = 256
LRU_GATE_ROWS = 512


def _sigmoid(x):
    return 0.5 * jnp.tanh(0.5 * x) + 0.5


def _vmem_limit(nbytes):
    return int(min(nbytes, VMEM_BYTES - 8 * 1024 * 1024))


def _mod_kernel(c_ref, w_ref, b_ref, lq1_ref, lk1_ref, lq2_ref, lk2_ref, mod_ref, lam_ref):
    c = c_ref[...]
    s = c * _sigmoid(c)
    mod_ref[...] = jnp.dot(s, w_ref[...], precision=lax.Precision.HIGHEST,
                           preferred_element_type=F32) + b_ref[...]
    s1 = jnp.sum(lq1_ref[...] * lk1_ref[...], axis=-1, keepdims=True)
    s2 = jnp.sum(lq2_ref[...] * lk2_ref[...], axis=-1, keepdims=True)
    lam = jnp.exp(s1) - jnp.exp(s2) + LAMBDA_INIT
    lam_ref[...] = jnp.broadcast_to(lam, lam_ref.shape)


def _modulation(c_all, w_mod, b_mod, lq1, lk1, lq2, lk2):
    d = c_all.shape[1]
    n_out = w_mod.shape[1]
    tn = n_out // 2
    vec = pl.BlockSpec((1, HEAD_DIM), lambda j: (0, 0))
    return pl.pallas_call(
        _mod_kernel,
        grid=(n_out // tn,),
        in_specs=[
            pl.BlockSpec((MOD_ROWS, d), lambda j: (0, 0)),
            pl.BlockSpec((d, tn), lambda j: (0, j)),
            pl.BlockSpec((1, tn), lambda j: (0, j)),
            vec, vec, vec, vec,
        ],
        out_specs=[
            pl.BlockSpec((MOD_ROWS, tn), lambda j: (0, j)),
            pl.BlockSpec((1, LANES), lambda j: (0, 0)),
        ],
        out_shape=[
            jax.ShapeDtypeStruct((MOD_ROWS, n_out), F32),
            jax.ShapeDtypeStruct((1, LANES), F32),
        ],
        name="mod",
    )(c_all, w_mod, b_mod, lq1, lk1, lq2, lk2)


def _rope_slab(xs, cos, sin, low_half):
    up = pltpu.roll(xs, LANES - 16, 1)
    dn = pltpu.roll(xs, 16, 1)
    return xs * cos + jnp.where(low_half, up, dn) * sin


def _proj_kernel(x_ref, mod_ref, g_ref, qcos_ref, qsin_ref, kcos_ref, ksin_ref, *w_and_out_refs,
                 d_model, groups):
    w_refs, out_refs = w_and_out_refs[:len(groups)], w_and_out_refs[len(groups):]
    x = x_ref[...]
    ms = jnp.mean(x * x, axis=-1, keepdims=True)
    y = x * lax.rsqrt(ms + NORM_EPS) * g_ref[...]
    shift = mod_ref[:, 0:d_model]
    scale = mod_ref[:, d_model:2 * d_model]
    h = (y * (1.0 + scale) + shift).astype(BF16)
    lane = lax.broadcasted_iota(jnp.int32, (x.shape[0], LANES), 1)
    low_half = (lane & 16) == 0
    for (width, rope, _, _), w_ref, o_ref in zip(groups, w_refs, out_refs):
        r = jnp.dot(h, w_ref[...], preferred_element_type=F32)
        if rope is None:
            o_ref[...] = r.astype(o_ref.dtype)
        else:
            cos_ref, sin_ref = (qcos_ref, qsin_ref) if rope == "q" else (kcos_ref, ksin_ref)
            cos = cos_ref[...]
            sin = sin_ref[...]
            for s in range(width // LANES):
                sl = slice(s * LANES, (s + 1) * LANES)
                o_ref[:, sl] = _rope_slab(r[:, sl], cos, sin, low_half).astype(o_ref.dtype)


def _project(x2, mod3, g_pre, w, tables, groups, rows_per_batch, mod_row_of_tile):
    n, d = x2.shape
    tm = min(PROJ_TM, rows_per_batch)
    tiles_per_batch = rows_per_batch // tm
    n_batches = n // rows_per_batch
    tab = pl.BlockSpec((tm, LANES), lambda i: (i % tiles_per_batch, 0))
    out_cols = sum(g[0] for g in groups)

    def out_spec(width, time_major):
        if time_major:
            return pl.BlockSpec((tm, width), lambda i: (i % tiles_per_batch, i // tiles_per_batch))
        return pl.BlockSpec((tm, width), lambda i: (i, 0))

    def out_shape(width, time_major):
        shape = (rows_per_batch, n_batches * width) if time_major else (n, width)
        return jax.ShapeDtypeStruct(shape, BF16)

    est = (d * out_cols * 2 + 2 * tm * d * 4 + 2 * tm * out_cols * 2 + 6 * tm * d * 4
           + 8 * tm * LANES * 4 + (4 << 20))

    def w_cols(width, offset):
        assert offset % width == 0
        return pl.BlockSpec((d, width), lambda i: (0, offset // width), pipeline_mode=pl.Buffered(1))

    return pl.pallas_call(
        functools.partial(_proj_kernel, d_model=d, groups=groups),
        grid=(n // tm,),
        in_specs=[
            pl.BlockSpec((tm, d), lambda i: (i, 0)),
            pl.BlockSpec((None, 1, 3 * d), lambda i: (mod_row_of_tile(i, tiles_per_batch), 0, 0)),
            pl.BlockSpec((1, d), lambda i: (0, 0)),
            tab, tab, tab, tab,
        ] + [w_cols(wd, off) for wd, _, off, _ in groups],
        out_specs=[out_spec(wd, tmaj) for wd, _, _, tmaj in groups],
        out_shape=[out_shape(wd, tmaj) for wd, _, _, tmaj in groups],
        compiler_params=pltpu.CompilerParams(
            dimension_semantics=("arbitrary",), vmem_limit_bytes=_vmem_limit(est)),
        name="proj",
    )(x2, mod3, g_pre, *tables, *([w] * len(groups)))


def _attn_kernel(q_ref, kc_ref, k_ref, vc_ref, v_ref, lam_ref, g_ref, o_ref, kall_ref, vext_ref,
                 s0_ref, s1_ref, *, n_ctx, rows):
    n_sub = q_ref.shape[0] // rows
    n_heads = q_ref.shape[1] // V_DIM
    cols = [slice(h * V_DIM, (h + 1) * V_DIM) for h in range(n_heads)]
    for h in range(n_heads):
        kall_ref[h, 0:n_ctx, :] = kc_ref[:, cols[h]]
        kall_ref[h, n_ctx:, :] = k_ref[:, cols[h]]
        vext_ref[h, 0:n_ctx, 0:V_DIM] = vc_ref[:, cols[h]]
        vext_ref[h, n_ctx:, 0:V_DIM] = v_ref[:, cols[h]]
        vext_ref[h, :, V_DIM:] = jnp.ones((vext_ref.shape[1], V_DIM), BF16)
    lane = lax.broadcasted_iota(jnp.int32, (rows, V_DIM), 1)
    first_half = lane < HEAD_DIM
    blocks = [(h, r) for h in range(n_heads) for r in range(n_sub)]
    s_refs = (s0_ref, s1_ref)

    def scores(idx):
        h, r = blocks[idx]
        q = q_ref[r * rows:(r + 1) * rows, cols[h]]
        zero = jnp.zeros_like(q)
        q2 = jnp.concatenate([jnp.where(first_half, q, zero), jnp.where(first_half, zero, q)], axis=0)
        s_refs[idx % 2][...] = lax.dot_general(q2, kall_ref[h], (((1,), (1,)), ((), ())),
                                              preferred_element_type=F32)

    def attend(idx):
        h, r = blocks[idx]
        s = s_refs[idx % 2][...]
        m = jnp.max(s, axis=-1, keepdims=True)
        e = jnp.exp2(s - m).astype(BF16)
        acc = jnp.dot(e, vext_ref[h], preferred_element_type=F32)
        oj = acc[:, 0:V_DIM] / acc[:, V_DIM:]
        o = oj[0:rows] - lam_ref[...] * oj[rows:]
        ms = jnp.mean(o * o, axis=-1, keepdims=True)
        o = o * lax.rsqrt(ms + NORM_EPS) * g_ref[...] * (1.0 - LAMBDA_INIT)
        o_ref[r * rows:(r + 1) * rows, cols[h]] = o.astype(o_ref.dtype)

    scores(0)
    for idx in range(len(blocks)):
        if idx + 1 < len(blocks):
            scores(idx + 1)
        attend(idx)


def _attention(q, kc, k, vc, v, lam_row, g_subln, first_batch, n_batch, n_lat, n_ctx):
    rows = ATTN_ROWS
    nh = ATTN_HEADS
    assert n_lat % rows == 0 and HEADS % nh == 0
    n_keys = n_ctx + n_lat
    est = (nh * (2 * (n_keys * V_DIM * 2 * 2) + n_keys * V_DIM * 2 + n_keys * 2 * V_DIM * 2
                 + 4 * n_lat * V_DIM * 2) + 2 * (2 * rows * n_keys * 4) + 4 * (2 * rows * n_keys * 4) + (8 << 20))

    def heads(n_rows):
        return pl.BlockSpec((n_rows, nh * V_DIM), lambda b, h: (b + first_batch, h))

    return pl.pallas_call(
        functools.partial(_attn_kernel, n_ctx=n_ctx, rows=rows),
        grid=(n_batch, HEADS // nh),
        in_specs=[
            heads(n_lat), heads(n_ctx), heads(n_lat), heads(n_ctx), heads(n_lat),
            pl.BlockSpec((1, LANES), lambda b, h: (0, 0)),
            pl.BlockSpec((1, V_DIM), lambda b, h: (0, 0)),
        ],
        out_specs=pl.BlockSpec((n_lat, nh * V_DIM), lambda b, h: (b, h)),
        out_shape=jax.ShapeDtypeStruct((n_batch * n_lat, HEADS * V_DIM), BF16),
        scratch_shapes=[
            pltpu.VMEM((nh, n_keys, V_DIM), BF16),
            pltpu.VMEM((nh, n_keys, 2 * V_DIM), BF16),
            pltpu.VMEM((2 * rows, n_keys), F32),
            pltpu.VMEM((2 * rows, n_keys), F32),
        ],
        compiler_params=pltpu.CompilerParams(
            dimension_semantics=("arbitrary", "arbitrary"),
            vmem_limit_bytes=_vmem_limit(est)),
        name="attn",
    )(q, kc, k, vc, v, lam_row, g_subln)


def _scan_kernel(xf_ref, xfp_ref, xfn_ref, xb_ref, xbp_ref, xbn_ref, cw_ref, cb_ref, wg_ref, bg_ref,
                 lam_ref, h0_ref, hf_ref, hb_ref, hfin_ref, xe_ref, wg_scr, a_ref, u_ref, st_ref,
                 *, batch, tc, wc):
    i = pl.program_id(1)
    n = pl.num_programs(1)
    t_grp = LRU_GATE_ROWS // batch
    n_blk, blk = wg_ref.shape[2], wg_ref.shape[3]

    @pl.when(i == 0)
    def _():
        st_ref[...] = h0_ref[...]
        wg_scr[...] = jnp.zeros(wg_scr.shape, BF16)
        for d in range(2):
            for gate in range(2):
                for m in range(n_blk):
                    wg_scr[d, m * blk:(m + 1) * blk, gate * wc + m * blk:gate * wc + (m + 1) * blk] = (
                        wg_ref[d, gate, m].astype(BF16))

    half_taps = [(0.5 * cw_ref[j:j + 1, :]).astype(BF16) for j in range(CONV_WIDTH)]
    half_cb = (0.5 * cb_ref[...]).astype(BF16)

    def coeffs(d, cur_ref, prev_ref, next_ref, chunk):
        prev = prev_ref[...]
        nxt = next_ref[...]
        xe_ref[d, 1:tc + 1] = cur_ref[...]
        xe_ref[d, 0:1] = jnp.where(chunk == 0, jnp.zeros_like(prev), prev)
        xe_ref[d, tc + 1:tc + 3] = jnp.where(chunk == n - 1, jnp.zeros_like(nxt), nxt)
        half_cl = (-0.5 * LRU_C * math.log2(math.e)) * jax.nn.softplus(-lam_ref[d])
        half_bg = 0.5 * bg_ref[d]
        for t0 in range(0, tc, t_grp):
            xh = half_cb
            for j in range(CONV_WIDTH):
                xh = xh + xe_ref[d, t0 + j:t0 + j + t_grp] * half_taps[j]
            xh = xh.reshape(t_grp * batch, wc)
            z = jnp.dot(xh, wg_scr[d], preferred_element_type=F32) + half_bg
            xh = xh.astype(F32)
            a = jnp.exp2(half_cl * jnp.tanh(z[:, 0:wc]) + half_cl)
            y = 1.0 - a * a
            mult = y * lax.rsqrt(jnp.maximum(y, 1e-30))
            u = (mult * xh) * (jnp.tanh(z[:, wc:]) + 1.0)
            a_ref[d, t0:t0 + t_grp] = a.reshape(t_grp, batch, wc)
            u_ref[d, t0:t0 + t_grp] = u.reshape(t_grp, batch, wc)

    coeffs(0, xf_ref, xfp_ref, xfn_ref, i)
    coeffs(1, xb_ref, xbp_ref, xbn_ref, n - 1 - i)

    def scan_step(t, hs):
        tb = tc - 1 - t
        h_f = a_ref[0, t] * hs[0] + u_ref[0, t]
        hf_ref[t] = h_f.astype(hf_ref.dtype)
        h_b = a_ref[1, tb] * hs[1] + u_ref[1, tb]
        hb_ref[tb] = h_b.astype(hb_ref.dtype)
        return h_f, h_b

    h_f, h_b = lax.fori_loop(0, tc, scan_step, (st_ref[0], st_ref[1]), unroll=8)
    st_ref[0] = h_f
    st_ref[1] = h_b

    @pl.when(i == n - 1)
    def _():
        hfin_ref[...] = st_ref[...]


def _scan(x_t, conv_w, conv_b, wg, bg, lam, h0):
    length, batch, width = x_t.shape
    tc, wc = LRU_TC, LRU_WC
    n = length // tc
    blk = wg.shape[-1]
    assert tc % 2 == 0 and LRU_GATE_ROWS % batch == 0 and tc % (LRU_GATE_ROWS // batch) == 0
    assert wc % blk == 0 and wg.shape[2] * blk == width

    def chunk(fwd, i):
        return i if fwd else n - 1 - i

    def cur(fwd):
        return pl.BlockSpec((tc, batch, wc), lambda s, i: (chunk(fwd, i), 0, s))

    def prev(fwd):
        return pl.BlockSpec((1, batch, wc), lambda s, i: (jnp.maximum(chunk(fwd, i) * tc - 1, 0), 0, s))

    def nxt(fwd):
        return pl.BlockSpec((2, batch, wc), lambda s, i: (
            jnp.minimum((chunk(fwd, i) + 1) * (tc // 2), length // 2 - 1), 0, s))

    chunk_f32 = 2 * tc * batch * wc * 4
    est = (2 * (tc + 3) * batch * wc * 4 + 2 * chunk_f32 + 8 * tc * batch * wc * 2
           + 8 * LRU_GATE_ROWS * 2 * wc * 4 + (8 << 20))
    return pl.pallas_call(
        functools.partial(_scan_kernel, batch=batch, tc=tc, wc=wc),
        grid=(width // wc, n),
        in_specs=[
            cur(True), prev(True), nxt(True), cur(False), prev(False), nxt(False),
            pl.BlockSpec((CONV_WIDTH, wc), lambda s, i: (0, s)),
            pl.BlockSpec((1, wc), lambda s, i: (0, s)),
            pl.BlockSpec((2, 2, wc // blk, blk, blk), lambda s, i: (0, 0, s, 0, 0)),
            pl.BlockSpec((2, None, 1, 2 * wc), lambda s, i: (0, s, 0, 0)),
            pl.BlockSpec((2, 1, wc), lambda s, i: (0, 0, s)),
            pl.BlockSpec((2, batch, wc), lambda s, i: (0, 0, s)),
        ],
        out_specs=[cur(True), cur(False), pl.BlockSpec((2, batch, wc), lambda s, i: (0, 0, s))],
        out_shape=[
            jax.ShapeDtypeStruct((length, batch, width), BF16),
            jax.ShapeDtypeStruct((length, batch, width), BF16),
            jax.ShapeDtypeStruct((2, batch, width), F32),
        ],
        scratch_shapes=[
            pltpu.VMEM((2, tc + 3, batch, wc), BF16),
            pltpu.VMEM((2, wc, 2 * wc), BF16),
            pltpu.VMEM((2, tc, batch, wc), F32),
            pltpu.VMEM((2, tc, batch, wc), F32),
            pltpu.VMEM((2, batch, wc), F32),
        ],
        compiler_params=pltpu.CompilerParams(
            dimension_semantics=("arbitrary", "arbitrary"), vmem_limit_bytes=_vmem_limit(est)),
        name="lru",
    )(x_t, x_t, x_t, x_t, x_t, x_t, conv_w, conv_b, wg, bg, lam, h0)


def _merge_kernel(attn_lo_ref, attn_hi_ref, ga_ref, hf_ref, hb_ref, gr_ref, gm_ref, x_ref, mod_ref, gp_ref,
                  wa_ref, wl_ref, wo_ref, o_ref, *, d_model, n_lo):
    def silu(ref):
        h = ref[...] * 0.5
        return h + h * jnp.tanh(h)

    attn = jnp.where(pl.program_id(0) < n_lo, attn_lo_ref[...], attn_hi_ref[...])
    a_in = attn * silu(ga_ref)
    y_attn = jnp.dot(a_in, wa_ref[...], preferred_element_type=F32)
    l_in = (hf_ref[...] + hb_ref[...]) * silu(gr_ref)
    y_lru = jnp.dot(l_in, wl_ref[...], preferred_element_type=F32)
    t_attn = jnp.tanh(0.5 * gm_ref[:, 0:d_model].astype(F32))
    t_lru = jnp.tanh(0.5 * gm_ref[:, d_model:].astype(F32))
    z = (0.5 * ((t_attn + 1.0) * y_attn + (t_lru + 1.0) * y_lru)).astype(BF16)
    y = jnp.dot(z, wo_ref[...], preferred_element_type=F32)
    ms = jnp.mean(y * y, axis=-1, keepdims=True)
    yn = y * lax.rsqrt(ms + NORM_EPS) * gp_ref[...]
    gate = mod_ref[:, 2 * d_model:]
    o_ref[...] = x_ref[...] + gate * yn


def _merge(attn_lo, attn_hi, ga, hf, hb, gr, gm, x2, mod3, g_post, wa, wl, wo, rows_per_batch):
    n, d = x2.shape
    tm = MERGE_TM
    tiles_per_batch = rows_per_batch // tm
    n_lo = attn_lo.shape[0] // tm
    n_hi = attn_hi.shape[0] // tm
    lo = pl.BlockSpec((tm, d), lambda i: (jnp.minimum(i, n_lo - 1), 0))
    hi = pl.BlockSpec((tm, d), lambda i: (jnp.clip(i - n_lo, 0, n_hi - 1), 0))
    row = pl.BlockSpec((tm, d), lambda i: (i, 0))
    trow = pl.BlockSpec((tm, d), lambda i: (i % tiles_per_batch, i // tiles_per_batch))
    wspec = pl.BlockSpec((d, d), lambda i: (0, 0), pipeline_mode=pl.Buffered(1))
    est = 3 * d * d * 2 + 2 * (6 * tm * d * 2 + tm * 2 * d * 2 + 2 * tm * d * 4) + 10 * tm * d * 4 + (4 << 20)
    return pl.pallas_call(
        functools.partial(_merge_kernel, d_model=d, n_lo=n_lo),
        grid=(n // tm,),
        in_specs=[
            lo, hi, row, trow, trow, row,
            pl.BlockSpec((tm, 2 * d), lambda i: (i, 0)),
            row,
            pl.BlockSpec((None, 1, 3 * d), lambda i: (i // tiles_per_batch, 0, 0)),
            pl.BlockSpec((1, d), lambda i: (0, 0)),
            wspec, wspec, wspec,
        ],
        out_specs=row,
        out_shape=jax.ShapeDtypeStruct((n, d), F32),
        compiler_params=pltpu.CompilerParams(
            dimension_semantics=("arbitrary",), vmem_limit_bytes=_vmem_limit(est)),
        name="merge",
    )(attn_lo, attn_hi, ga, hf, hb, gr, gm, x2, mod3, g_post, wa, wl, wo)


def _rope_tables(n_tokens):
    t = np.arange(n_tokens)
    row = (t // GRID_W).astype(np.float64)
    col = (t % GRID_W).astype(np.float64)
    axis_dim = HEAD_DIM // 2
    inv_freq = ROPE_THETA ** (-np.arange(0, axis_dim, 2, dtype=np.float64) / axis_dim)
    d = np.arange(LANES) % HEAD_DIM
    freq = inv_freq[d % (axis_dim // 2)]
    pos = np.where((d < axis_dim)[None, :], row[:, None], col[:, None])
    ang = pos * freq[None, :]
    sign = np.where((d % axis_dim) < axis_dim // 2, -1.0, 1.0)
    cos = np.cos(ang)
    sin = np.sin(ang) * sign[None, :]
    qs = (HEAD_DIM ** -0.5) * math.log2(math.e)
    return tuple(jnp.asarray(a, F32) for a in (cos * qs, sin * qs, cos, sin))


def _gate_params(w_a, w_x, b_a, b_x, wc):
    n_slab = b_a.shape[-1] // wc
    wg = jnp.stack([w_a, w_x], axis=1)
    bg = jnp.concatenate([b_a.reshape(2, n_slab, 1, wc), b_x.reshape(2, n_slab, 1, wc)], axis=-1)
    return wg, bg


def kernel(x, c, ctx, c_ctx, w_mod, b_mod, g_pre, g_post, w_in, lambda_q1, lambda_k1, lambda_q2, lambda_k2,
           g_subln, w_attn_out, conv_w, conv_b, w_rg_a, b_rg_a, w_rg_x, b_rg_x, lru_lambda, w_lru_out, w_out):
    batch, n_lat, d = x.shape
    n_ctx = ctx.shape[1]
    assert w_mod.shape[0] == 1, "single-layer block"
    assert d == HEADS * V_DIM and n_lat % PROJ_TM == 0 and n_lat % MERGE_TM == 0 and n_ctx % BF16_ROWS == 0

    c_all = jnp.zeros((MOD_ROWS, d), F32).at[:batch].set(c).at[batch].set(c_ctx)
    vec = lambda a: a[0].reshape(1, HEAD_DIM)
    mod, lam_row = _modulation(c_all, w_mod[0], b_mod[0].reshape(1, -1),
                               vec(lambda_q1), vec(lambda_k1), vec(lambda_q2), vec(lambda_k2))
    mod3 = mod.reshape(MOD_ROWS, 1, 3 * d)

    w_bf = w_in[0].astype(BF16)
    tables = _rope_tables(n_lat)
    g_pre2 = g_pre[0].reshape(1, d)
    x2 = x.reshape(batch * n_lat, d)
    lat_groups = ((d, "q", 0, False), (d, "k", d, False), (d, None, 2 * d, False), (d, None, 3 * d, False),
                  (d, None, 4 * d, True), (d, None, 5 * d, False), (2 * d, None, 6 * d, False))
    q, k, v, ga, xr, gr, gm = _project(
        x2, mod3, g_pre2, w_bf, tables, lat_groups, n_lat, lambda i, tpb: i // tpb)

    ctx_groups = ((d, None, d, False), (d, None, 2 * d, False), (d, None, 4 * d, True))
    kc, vc, xrc = _project(
        ctx.reshape(batch * n_ctx, d), mod3, g_pre2, w_bf, tables, ctx_groups, n_ctx,
        lambda i, tpb: batch)

    half = batch // 2
    attn_args = (q, kc, k, vc, v, lam_row, g_subln[0].reshape(1, V_DIM))
    attn_lo = _attention(*attn_args, 0, half, n_lat, n_ctx)

    wg, bg = _gate_params(w_rg_a[0], w_rg_x[0], b_rg_a[0], b_rg_x[0], LRU_WC)
    lam = lru_lambda[0].reshape(2, 1, d)
    cb = conv_b[0].reshape(1, d)
    zeros = jnp.zeros((2, batch, d), F32)
    _, _, h_ctx = _scan(xrc.reshape(n_ctx, batch, d), conv_w[0], cb, wg, bg, lam, zeros)
    hf, hb, _ = _scan(xr.reshape(n_lat, batch, d), conv_w[0], cb, wg, bg, lam, h_ctx)

    attn_hi = _attention(*attn_args, half, batch - half, n_lat, n_ctx)

    out = _merge(attn_lo, attn_hi, ga, hf.reshape(n_lat, batch * d), hb.reshape(n_lat, batch * d), gr, gm, x2, mod3,
                 g_post[0].reshape(1, d), w_attn_out[0].astype(BF16), w_lru_out[0].astype(BF16),
                 w_out[0].astype(BF16), n_lat)
    return out.reshape(batch, n_lat, d)
```

```python
import functools
import math

import jax
import jax.numpy as jnp
import numpy as np
from jax import lax
from jax.experimental import pallas as pl
from jax.experimental.pallas import tpu as pltpu

F32 = jnp.float32
BF16 = jnp.bfloat16

LANES = 128
SUBLANES = 8
BF16_ROWS = 16
VMEM_BYTES = 64 * 1024 * 1024

GRID_W = 64
HEADS = 8
HEAD_DIM = 64
V_DIM = 2 * HEAD_DIM
CONV_WIDTH = 4
LRU_C = 8.0
ROPE_THETA = 10000.0
NORM_EPS = 1e-6
LAMBDA_INIT = 0.8 - 0.6 * math.exp(-0.3 * 0)

MOD_ROWS = 24
PROJ_TM = 512
MERGE_TM = 512
MERGE_IN_BUFFERS = 3
ATTN_ROWS = 128
ATTN_HEADS = 2
LRU_TC = 256
LRU_WC = 256
LRU_GATE_ROWS = 512


def _sigmoid(x):
    return 0.5 * jnp.tanh(0.5 * x) + 0.5


def _vmem_limit(nbytes):
    return int(min(nbytes, VMEM_BYTES - 8 * 1024 * 1024))


def _mod_kernel(c_ref, w_ref, b_ref, lq1_ref, lk1_ref, lq2_ref, lk2_ref, mod_ref, lam_ref):
    c = c_ref[...]
    s = c * _sigmoid(c)
    mod_ref[...] = jnp.dot(s, w_ref[...], precision=lax.Precision.HIGHEST,
                           preferred_element_type=F32) + b_ref[...]
    s1 = jnp.sum(lq1_ref[...] * lk1_ref[...], axis=-1, keepdims=True)
    s2 = jnp.sum(lq2_ref[...] * lk2_ref[...], axis=-1, keepdims=True)
    lam = jnp.exp(s1) - jnp.exp(s2) + LAMBDA_INIT
    lam_ref[...] = jnp.broadcast_to(lam, lam_ref.shape)


def _modulation(c_all, w_mod, b_mod, lq1, lk1, lq2, lk2):
    d = c_all.shape[1]
    n_out = w_mod.shape[1]
    tn = n_out // 2
    vec = pl.BlockSpec((1, HEAD_DIM), lambda j: (0, 0))
    return pl.pallas_call(
        _mod_kernel,
        grid=(n_out // tn,),
        in_specs=[
            pl.BlockSpec((MOD_ROWS, d), lambda j: (0, 0)),
            pl.BlockSpec((d, tn), lambda j: (0, j)),
            pl.BlockSpec((1, tn), lambda j: (0, j)),
            vec, vec, vec, vec,
        ],
        out_specs=[
            pl.BlockSpec((MOD_ROWS, tn), lambda j: (0, j)),
            pl.BlockSpec((1, LANES), lambda j: (0, 0)),
        ],
        out_shape=[
            jax.ShapeDtypeStruct((MOD_ROWS, n_out), F32),
            jax.ShapeDtypeStruct((1, LANES), F32),
        ],
        name="mod",
    )(c_all, w_mod, b_mod, lq1, lk1, lq2, lk2)


def _rope_slab(xs, cos, sin, low_half):
    up = pltpu.roll(xs, LANES - 16, 1)
    dn = pltpu.roll(xs, 16, 1)
    return xs * cos + jnp.where(low_half, up, dn) * sin


def _proj_kernel(x_ref, mod_ref, g_ref, qcos_ref, qsin_ref, kcos_ref, ksin_ref, *w_and_out_refs,
                 d_model, groups):
    w_refs, out_refs = w_and_out_refs[:len(groups)], w_and_out_refs[len(groups):]
    x = x_ref[...]
    ms = jnp.mean(x * x, axis=-1, keepdims=True)
    y = x * lax.rsqrt(ms + NORM_EPS) * g_ref[...]
    shift = mod_ref[:, 0:d_model]
    scale = mod_ref[:, d_model:2 * d_model]
    h = (y * (1.0 + scale) + shift).astype(BF16)
    lane = lax.broadcasted_iota(jnp.int32, (x.shape[0], LANES), 1)
    low_half = (lane & 16) == 0
    for (width, rope, _, _), w_ref, o_ref in zip(groups, w_refs, out_refs):
        r = jnp.dot(h, w_ref[...], preferred_element_type=F32)
        if rope is None:
            o_ref[...] = r.astype(o_ref.dtype)
        else:
            cos_ref, sin_ref = (qcos_ref, qsin_ref) if rope == "q" else (kcos_ref, ksin_ref)
            cos = cos_ref[...]
            sin = sin_ref[...]
            for s in range(width // LANES):
                sl = slice(s * LANES, (s + 1) * LANES)
                o_ref[:, sl] = _rope_slab(r[:, sl], cos, sin, low_half).astype(o_ref.dtype)


def _project(x2, mod3, g_pre, w, tables, groups, rows_per_batch, mod_row_of_tile):
    n, d = x2.shape
    tm = min(PROJ_TM, rows_per_batch)
    tiles_per_batch = rows_per_batch // tm
    n_batches = n // rows_per_batch
    tab = pl.BlockSpec((tm, LANES), lambda i: (i % tiles_per_batch, 0))
    out_cols = sum(g[0] for g in groups)

    def out_spec(width, time_major):
        if time_major:
            return pl.BlockSpec((tm, width), lambda i: (i % tiles_per_batch, i // tiles_per_batch))
        return pl.BlockSpec((tm, width), lambda i: (i, 0))

    def out_shape(width, time_major):
        shape = (rows_per_batch, n_batches * width) if time_major else (n, width)
        return jax.ShapeDtypeStruct(shape, BF16)

    est = (d * out_cols * 2 + 2 * tm * d * 4 + 2 * tm * out_cols * 2 + 6 * tm * d * 4
           + 8 * tm * LANES * 4 + (4 << 20))

    def w_cols(width, offset):
        assert offset % width == 0
        return pl.BlockSpec((d, width), lambda i: (0, offset // width), pipeline_mode=pl.Buffered(1))

    return pl.pallas_call(
        functools.partial(_proj_kernel, d_model=d, groups=groups),
        grid=(n // tm,),
        in_specs=[
            pl.BlockSpec((tm, d), lambda i: (i, 0)),
            pl.BlockSpec((None, 1, 3 * d), lambda i: (mod_row_of_tile(i, tiles_per_batch), 0, 0)),
            pl.BlockSpec((1, d), lambda i: (0, 0)),
            tab, tab, tab, tab,
        ] + [w_cols(wd, off) for wd, _, off, _ in groups],
        out_specs=[out_spec(wd, tmaj) for wd, _, _, tmaj in groups],
        out_shape=[out_shape(wd, tmaj) for wd, _, _, tmaj in groups],
        compiler_params=pltpu.CompilerParams(
            dimension_semantics=("arbitrary",), vmem_limit_bytes=_vmem_limit(est)),
        name="proj",
    )(x2, mod3, g_pre, *tables, *([w] * len(groups)))


def _attn_kernel(q_ref, kc_ref, k_ref, vc_ref, v_ref, lam_ref, g_ref, o_ref, kall_ref, vext_ref,
                 s0_ref, s1_ref, *, n_ctx, rows):
    n_sub = q_ref.shape[0] // rows
    n_heads = q_ref.shape[1] // V_DIM
    cols = [slice(h * V_DIM, (h + 1) * V_DIM) for h in range(n_heads)]
    for h in range(n_heads):
        kall_ref[h, 0:n_ctx, :] = kc_ref[:, cols[h]]
        kall_ref[h, n_ctx:, :] = k_ref[:, cols[h]]
        vext_ref[h, 0:n_ctx, 0:V_DIM] = vc_ref[:, cols[h]]
        vext_ref[h, n_ctx:, 0:V_DIM] = v_ref[:, cols[h]]
        vext_ref[h, :, V_DIM:] = jnp.ones((vext_ref.shape[1], V_DIM), BF16)
    lane = lax.broadcasted_iota(jnp.int32, (rows, V_DIM), 1)
    first_half = lane < HEAD_DIM
    blocks = [(h, r) for h in range(n_heads) for r in range(n_sub)]
    s_refs = (s0_ref, s1_ref)

    def scores(idx):
        h, r = blocks[idx]
        q = q_ref[r * rows:(r + 1) * rows, cols[h]]
        zero = jnp.zeros_like(q)
        q2 = jnp.concatenate([jnp.where(first_half, q, zero), jnp.where(first_half, zero, q)], axis=0)
        s_refs[idx % 2][...] = lax.dot_general(q2, kall_ref[h], (((1,), (1,)), ((), ())),
                                              preferred_element_type=F32)

    def attend(idx):
        h, r = blocks[idx]
        s = s_refs[idx % 2][...]
        m = jnp.max(s, axis=-1, keepdims=True)
        e = jnp.exp2(s - m).astype(BF16)
        acc = jnp.dot(e, vext_ref[h], preferred_element_type=F32)
        oj = acc[:, 0:V_DIM] / acc[:, V_DIM:]
        o = oj[0:rows] - lam_ref[...] * oj[rows:]
        ms = jnp.mean(o * o, axis=-1, keepdims=True)
        o = o * lax.rsqrt(ms + NORM_EPS) * g_ref[...] * (1.0 - LAMBDA_INIT)
        o_ref[r * rows:(r + 1) * rows, cols[h]] = o.astype(o_ref.dtype)

    scores(0)
    for idx in range(len(blocks)):
        if idx + 1 < len(blocks):
            scores(idx + 1)
        attend(idx)


def _attention(q, kc, k, vc, v, lam_row, g_subln, batch, n_lat, n_ctx):
    rows = ATTN_ROWS
    nh = ATTN_HEADS
    assert n_lat % rows == 0 and HEADS % nh == 0
    n_keys = n_ctx + n_lat
    est = (nh * (2 * (n_keys * V_DIM * 2 * 2) + n_keys * V_DIM * 2 + n_keys * 2 * V_DIM * 2
                 + 4 * n_lat * V_DIM * 2) + 2 * (2 * rows * n_keys * 4) + 4 * (2 * rows * n_keys * 4) + (8 << 20))

    def heads(n_rows):
        return pl.BlockSpec((n_rows, nh * V_DIM), lambda b, h: (b, h))

    return pl.pallas_call(
        functools.partial(_attn_kernel, n_ctx=n_ctx, rows=rows),
        grid=(batch, HEADS // nh),
        in_specs=[
            heads(n_lat), heads(n_ctx), heads(n_lat), heads(n_ctx), heads(n_lat),
            pl.BlockSpec((1, LANES), lambda b, h: (0, 0)),
            pl.BlockSpec((1, V_DIM), lambda b, h: (0, 0)),
        ],
        out_specs=heads(n_lat),
        out_shape=jax.ShapeDtypeStruct((batch * n_lat, HEADS * V_DIM), BF16),
        scratch_shapes=[
            pltpu.VMEM((nh, n_keys, V_DIM), BF16),
            pltpu.VMEM((nh, n_keys, 2 * V_DIM), BF16),
            pltpu.VMEM((2 * rows, n_keys), F32),
            pltpu.VMEM((2 * rows, n_keys), F32),
        ],
        compiler_params=pltpu.CompilerParams(
            dimension_semantics=("arbitrary", "arbitrary"),
            vmem_limit_bytes=_vmem_limit(est)),
        name="attn",
    )(q, kc, k, vc, v, lam_row, g_subln)


def _scan_kernel(xf_ref, xfp_ref, xfn_ref, xb_ref, xbp_ref, xbn_ref, cw_ref, cb_ref, wg_ref, bg_ref,
                 lam_ref, h0_ref, hf_ref, hb_ref, hfin_ref, xe_ref, wg_scr, a_ref, u_ref, st_ref,
                 *, batch, tc, wc):
    i = pl.program_id(1)
    n = pl.num_programs(1)
    t_grp = LRU_GATE_ROWS // batch
    n_blk, blk = wg_ref.shape[2], wg_ref.shape[3]

    @pl.when(i == 0)
    def _():
        st_ref[...] = h0_ref[...]
        wg_scr[...] = jnp.zeros(wg_scr.shape, BF16)
        for d in range(2):
            for gate in range(2):
                for m in range(n_blk):
                    wg_scr[d, m * blk:(m + 1) * blk, gate * wc + m * blk:gate * wc + (m + 1) * blk] = (
                        wg_ref[d, gate, m].astype(BF16))

    half_taps = [(0.5 * cw_ref[j:j + 1, :]).astype(BF16) for j in range(CONV_WIDTH)]
    half_cb = (0.5 * cb_ref[...]).astype(BF16)

    def coeffs(d, cur_ref, prev_ref, next_ref, chunk):
        prev = prev_ref[...]
        nxt = next_ref[...]
        xe_ref[d, 1:tc + 1] = cur_ref[...]
        xe_ref[d, 0:1] = jnp.where(chunk == 0, jnp.zeros_like(prev), prev)
        xe_ref[d, tc + 1:tc + 3] = jnp.where(chunk == n - 1, jnp.zeros_like(nxt), nxt)
        half_cl = (-0.5 * LRU_C * math.log2(math.e)) * jax.nn.softplus(-lam_ref[d])
        half_bg = 0.5 * bg_ref[d]
        for t0 in range(0, tc, t_grp):
            xh = half_cb
            for j in range(CONV_WIDTH):
                xh = xh + xe_ref[d, t0 + j:t0 + j + t_grp] * half_taps[j]
            xh = xh.reshape(t_grp * batch, wc)
            z = jnp.dot(xh, wg_scr[d], preferred_element_type=F32) + half_bg
            xh = xh.astype(F32)
            a = jnp.exp2(half_cl * jnp.tanh(z[:, 0:wc]) + half_cl)
            y = 1.0 - a * a
            mult = y * lax.rsqrt(jnp.maximum(y, 1e-30))
            u = (mult * xh) * (jnp.tanh(z[:, wc:]) + 1.0)
            a_ref[d, t0:t0 + t_grp] = a.reshape(t_grp, batch, wc)
            u_ref[d, t0:t0 + t_grp] = u.reshape(t_grp, batch, wc)

    coeffs(0, xf_ref, xfp_ref, xfn_ref, i)
    coeffs(1, xb_ref, xbp_ref, xbn_ref, n - 1 - i)

    def scan_step(t, hs):
        tb = tc - 1 - t
        h_f = a_ref[0, t] * hs[0] + u_ref[0, t]
        hf_ref[t] = h_f.astype(hf_ref.dtype)
        h_b = a_ref[1, tb] * hs[1] + u_ref[1, tb]
        hb_ref[tb] = h_b.astype(hb_ref.dtype)
        return h_f, h_b

    h_f, h_b = lax.fori_loop(0, tc, scan_step, (st_ref[0], st_ref[1]), unroll=8)
    st_ref[0] = h_f
    st_ref[1] = h_b

    @pl.when(i == n - 1)
    def _():
        hfin_ref[...] = st_ref[...]


def _scan(x_t, conv_w, conv_b, wg, bg, lam, h0):
    length, batch, width = x_t.shape
    tc, wc = LRU_TC, LRU_WC
    n = length // tc
    blk = wg.shape[-1]
    assert tc % 2 == 0 and LRU_GATE_ROWS % batch == 0 and tc % (LRU_GATE_ROWS // batch) == 0
    assert wc % blk == 0 and wg.shape[2] * blk == width

    def chunk(fwd, i):
        return i if fwd else n - 1 - i

    def cur(fwd):
        return pl.BlockSpec((tc, batch, wc), lambda s, i: (chunk(fwd, i), 0, s))

    def prev(fwd):
        return pl.BlockSpec((1, batch, wc), lambda s, i: (jnp.maximum(chunk(fwd, i) * tc - 1, 0), 0, s))

    def nxt(fwd):
        return pl.BlockSpec((2, batch, wc), lambda s, i: (
            jnp.minimum((chunk(fwd, i) + 1) * (tc // 2), length // 2 - 1), 0, s))

    chunk_f32 = 2 * tc * batch * wc * 4
    est = (2 * (tc + 3) * batch * wc * 4 + 2 * chunk_f32 + 8 * tc * batch * wc * 2
           + 8 * LRU_GATE_ROWS * 2 * wc * 4 + (8 << 20))
    return pl.pallas_call(
        functools.partial(_scan_kernel, batch=batch, tc=tc, wc=wc),
        grid=(width // wc, n),
        in_specs=[
            cur(True), prev(True), nxt(True), cur(False), prev(False), nxt(False),
            pl.BlockSpec((CONV_WIDTH, wc), lambda s, i: (0, s)),
            pl.BlockSpec((1, wc), lambda s, i: (0, s)),
            pl.BlockSpec((2, 2, wc // blk, blk, blk), lambda s, i: (0, 0, s, 0, 0)),
            pl.BlockSpec((2, None, 1, 2 * wc), lambda s, i: (0, s, 0, 0)),
            pl.BlockSpec((2, 1, wc), lambda s, i: (0, 0, s)),
            pl.BlockSpec((2, batch, wc), lambda s, i: (0, 0, s)),
        ],
        out_specs=[cur(True), cur(False), pl.BlockSpec((2, batch, wc), lambda s, i: (0, 0, s))],
        out_shape=[
            jax.ShapeDtypeStruct((length, batch, width), BF16),
            jax.ShapeDtypeStruct((length, batch, width), BF16),
            jax.ShapeDtypeStruct((2, batch, width), F32),
        ],
        scratch_shapes=[
            pltpu.VMEM((2, tc + 3, batch, wc), BF16),
            pltpu.VMEM((2, wc, 2 * wc), BF16),
            pltpu.VMEM((2, tc, batch, wc), F32),
            pltpu.VMEM((2, tc, batch, wc), F32),
            pltpu.VMEM((2, batch, wc), F32),
        ],
        compiler_params=pltpu.CompilerParams(
            dimension_semantics=("arbitrary", "arbitrary"), vmem_limit_bytes=_vmem_limit(est)),
        name="lru",
    )(x_t, x_t, x_t, x_t, x_t, x_t, conv_w, conv_b, wg, bg, lam, h0)


def _merge_kernel(attn_ref, ga_ref, hf_ref, hb_ref, gr_ref, gm_ref, x_ref, mod_ref, gp_ref,
                  wa_ref, wl_ref, wo_ref, o_ref, *, d_model):
    def silu(ref):
        h = ref[...] * 0.5
        return h + h * jnp.tanh(h)

    a_in = attn_ref[...] * silu(ga_ref)
    y_attn = jnp.dot(a_in, wa_ref[...], preferred_element_type=F32)
    l_in = (hf_ref[...] + hb_ref[...]) * silu(gr_ref)
    y_lru = jnp.dot(l_in, wl_ref[...], preferred_element_type=F32)
    t_attn = jnp.tanh(0.5 * gm_ref[:, 0:d_model].astype(F32))
    t_lru = jnp.tanh(0.5 * gm_ref[:, d_model:].astype(F32))
    z = (0.5 * ((t_attn + 1.0) * y_attn + (t_lru + 1.0) * y_lru)).astype(BF16)
    y = jnp.dot(z, wo_ref[...], preferred_element_type=F32)
    ms = jnp.mean(y * y, axis=-1, keepdims=True)
    yn = y * lax.rsqrt(ms + NORM_EPS) * gp_ref[...]
    gate = mod_ref[:, 2 * d_model:]
    o_ref[...] = x_ref[...] + gate * yn


def _merge(attn, ga, hf, hb, gr, gm, x2, mod3, g_post, wa, wl, wo, rows_per_batch):
    n, d = x2.shape
    tm = MERGE_TM
    tiles_per_batch = rows_per_batch // tm
    deep = dict(pipeline_mode=pl.Buffered(MERGE_IN_BUFFERS))
    row = pl.BlockSpec((tm, d), lambda i: (i, 0), **deep)
    trow = pl.BlockSpec((tm, d), lambda i: (i % tiles_per_batch, i // tiles_per_batch), **deep)
    in_specs = [
        row, row, trow, trow, row,
        pl.BlockSpec((tm, 2 * d), lambda i: (i, 0), **deep),
        row,
        pl.BlockSpec((1, 1, 3 * d), lambda i: (i // tiles_per_batch, 0, 0)),
    ]
    out_spec = pl.BlockSpec((tm, d), lambda i: (i, 0))

    def call_kernel(attn_h, ga_h, hf_h, hb_h, gr_h, gm_h, x_h, mod_h, gp_ref, wa_ref, wl_ref, wo_ref, o_h):
        def step(attn_ref, ga_ref, hf_ref, hb_ref, gr_ref, gm_ref, x_ref, mod_ref, o_ref):
            _merge_kernel(attn_ref, ga_ref, hf_ref, hb_ref, gr_ref, gm_ref, x_ref, mod_ref.at[0], gp_ref,
                          wa_ref, wl_ref, wo_ref, o_ref, d_model=d)

        pltpu.emit_pipeline(step, grid=(n // tm,), in_specs=in_specs, out_specs=[out_spec])(
            attn_h, ga_h, hf_h, hb_h, gr_h, gm_h, x_h, mod_h, o_h)

    streamed = pl.BlockSpec(memory_space=pl.ANY)
    resident = pl.BlockSpec(memory_space=pltpu.VMEM)
    est = (3 * d * d * 2 + MERGE_IN_BUFFERS * (5 * tm * d * 2 + tm * 2 * d * 2 + tm * d * 4) + 2 * tm * d * 4
           + 10 * tm * d * 4 + (4 << 20))
    return pl.pallas_call(
        call_kernel,
        in_specs=[streamed] * 8 + [resident] * 4,
        out_specs=streamed,
        out_shape=jax.ShapeDtypeStruct((n, d), F32),
        compiler_params=pltpu.CompilerParams(vmem_limit_bytes=_vmem_limit(est)),
        name="merge",
    )(attn, ga, hf, hb, gr, gm, x2, mod3, g_post, wa, wl, wo)


def _rope_tables(n_tokens):
    t = np.arange(n_tokens)
    row = (t // GRID_W).astype(np.float64)
    col = (t % GRID_W).astype(np.float64)
    axis_dim = HEAD_DIM // 2
    inv_freq = ROPE_THETA ** (-np.arange(0, axis_dim, 2, dtype=np.float64) / axis_dim)
    d = np.arange(LANES) % HEAD_DIM
    freq = inv_freq[d % (axis_dim // 2)]
    pos = np.where((d < axis_dim)[None, :], row[:, None], col[:, None])
    ang = pos * freq[None, :]
    sign = np.where((d % axis_dim) < axis_dim // 2, -1.0, 1.0)
    cos = np.cos(ang)
    sin = np.sin(ang) * sign[None, :]
    qs = (HEAD_DIM ** -0.5) * math.log2(math.e)
    return tuple(jnp.asarray(a, F32) for a in (cos * qs, sin * qs, cos, sin))


def _gate_params(w_a, w_x, b_a, b_x, wc):
    n_slab = b_a.shape[-1] // wc
    wg = jnp.stack([w_a, w_x], axis=1)
    bg = jnp.concatenate([b_a.reshape(2, n_slab, 1, wc), b_x.reshape(2, n_slab, 1, wc)], axis=-1)
    return wg, bg


def kernel(x, c, ctx, c_ctx, w_mod, b_mod, g_pre, g_post, w_in, lambda_q1, lambda_k1, lambda_q2, lambda_k2,
           g_subln, w_attn_out, conv_w, conv_b, w_rg_a, b_rg_a, w_rg_x, b_rg_x, lru_lambda, w_lru_out, w_out):
    batch, n_lat, d = x.shape
    n_ctx = ctx.shape[1]
    assert w_mod.shape[0] == 1, "single-layer block"
    assert d == HEADS * V_DIM and n_lat % PROJ_TM == 0 and n_lat % MERGE_TM == 0 and n_ctx % BF16_ROWS == 0

    c_all = jnp.zeros((MOD_ROWS, d), F32).at[:batch].set(c).at[batch].set(c_ctx)
    vec = lambda a: a[0].reshape(1, HEAD_DIM)
    mod, lam_row = _modulation(c_all, w_mod[0], b_mod[0].reshape(1, -1),
                               vec(lambda_q1), vec(lambda_k1), vec(lambda_q2), vec(lambda_k2))
    mod3 = mod.reshape(MOD_ROWS, 1, 3 * d)

    w_bf = w_in[0].astype(BF16)
    tables = _rope_tables(n_lat)
    g_pre2 = g_pre[0].reshape(1, d)
    x2 = x.reshape(batch * n_lat, d)
    lat_groups = ((d, "q", 0, False), (d, "k", d, False), (d, None, 2 * d, False), (d, None, 3 * d, False),
                  (d, None, 4 * d, True), (d, None, 5 * d, False), (2 * d, None, 6 * d, False))
    q, k, v, ga, xr, gr, gm = _project(
        x2, mod3, g_pre2, w_bf, tables, lat_groups, n_lat, lambda i, tpb: i // tpb)

    ctx_groups = ((d, None, d, False), (d, None, 2 * d, False), (d, None, 4 * d, True))
    kc, vc, xrc = _project(
        ctx.reshape(batch * n_ctx, d), mod3, g_pre2, w_bf, tables, ctx_groups, n_ctx,
        lambda i, tpb: batch)

    attn = _attention(q, kc, k, vc, v, lam_row, g_subln[0].reshape(1, V_DIM), batch, n_lat, n_ctx)

    wg, bg = _gate_params(w_rg_a[0], w_rg_x[0], b_rg_a[0], b_rg_x[0], LRU_WC)
    lam = lru_lambda[0].reshape(2, 1, d)
    cb = conv_b[0].reshape(1, d)
    zeros = jnp.zeros((2, batch, d), F32)
    _, _, h_ctx = _scan(xrc.reshape(n_ctx, batch, d), conv_w[0], cb, wg, bg, lam, zeros)
    hf, hb, _ = _scan(xr.reshape(n_lat, batch, d), conv_w[0], cb, wg, bg, lam, h_ctx)

    out = _merge(attn, ga, hf.reshape(n_lat, batch * d), hb.reshape(n_lat, batch * d), gr, gm, x2, mod3,
                 g_post[0].reshape(1, d), w_attn_out[0].astype(BF16), w_lru_out[0].astype(BF16),
                 w_out[0].astype(BF16), n_lat)
    return out.reshape(batch, n_lat, d)
```

```python
import functools
import math

import jax
import jax.numpy as jnp
import numpy as np
from jax import lax
from jax.experimental import pallas as pl
from jax.experimental.pallas import tpu as pltpu

F32 = jnp.float32
BF16 = jnp.bfloat16

LANES = 128
SUBLANES = 8
BF16_ROWS = 16
VMEM_BYTES = 64 * 1024 * 1024

GRID_W = 64
HEADS = 8
HEAD_DIM = 64
V_DIM = 2 * HEAD_DIM
CONV_WIDTH = 4
LRU_C = 8.0
ROPE_THETA = 10000.0
NORM_EPS = 1e-6
LAMBDA_INIT = 0.8 - 0.6 * math.exp(-0.3 * 0)

MOD_ROWS = 24
PROJ_TM = 512
MERGE_TM = 512
ATTN_ROWS = 128
ATTN_HEADS = 2
LRU_TC = 256
LRU_WC = 256
LRU_GATE_ROWS = 512


def _sigmoid(x):
    return 0.5 * jnp.tanh(0.5 * x) + 0.5


def _vmem_limit(nbytes):
    return int(min(nbytes, VMEM_BYTES - 8 * 1024 * 1024))


def _mod_kernel(c_ref, w_ref, b_ref, lq1_ref, lk1_ref, lq2_ref, lk2_ref, mod_ref, lam_ref):
    c = c_ref[...]
    s = c * _sigmoid(c)
    mod_ref[...] = jnp.dot(s, w_ref[...], precision=lax.Precision.HIGHEST,
                           preferred_element_type=F32) + b_ref[...]
    s1 = jnp.sum(lq1_ref[...] * lk1_ref[...], axis=-1, keepdims=True)
    s2 = jnp.sum(lq2_ref[...] * lk2_ref[...], axis=-1, keepdims=True)
    lam = jnp.exp(s1) - jnp.exp(s2) + LAMBDA_INIT
    lam_ref[...] = jnp.broadcast_to(lam, lam_ref.shape)


def _modulation(c_all, w_mod, b_mod, lq1, lk1, lq2, lk2):
    d = c_all.shape[1]
    n_out = w_mod.shape[1]
    tn = n_out // 2
    vec = pl.BlockSpec((1, HEAD_DIM), lambda j: (0, 0))
    return pl.pallas_call(
        _mod_kernel,
        grid=(n_out // tn,),
        in_specs=[
            pl.BlockSpec((MOD_ROWS, d), lambda j: (0, 0)),
            pl.BlockSpec((d, tn), lambda j: (0, j)),
            pl.BlockSpec((1, tn), lambda j: (0, j)),
            vec, vec, vec, vec,
        ],
        out_specs=[
            pl.BlockSpec((MOD_ROWS, tn), lambda j: (0, j)),
            pl.BlockSpec((1, LANES), lambda j: (0, 0)),
        ],
        out_shape=[
            jax.ShapeDtypeStruct((MOD_ROWS, n_out), F32),
            jax.ShapeDtypeStruct((1, LANES), F32),
        ],
        name="mod",
    )(c_all, w_mod, b_mod, lq1, lk1, lq2, lk2)


def _rope_slab(xs, cos, sin, low_half):
    up = pltpu.roll(xs, LANES - 16, 1)
    dn = pltpu.roll(xs, 16, 1)
    return xs * cos + jnp.where(low_half, up, dn) * sin


def _proj_kernel(x_ref, mod_ref, g_ref, qcos_ref, qsin_ref, kcos_ref, ksin_ref, *w_and_out_refs,
                 d_model, groups):
    w_refs, out_refs = w_and_out_refs[:len(groups)], w_and_out_refs[len(groups):]
    x = x_ref[...]
    ms = jnp.mean(x * x, axis=-1, keepdims=True)
    y = x * lax.rsqrt(ms + NORM_EPS) * g_ref[...]
    shift = mod_ref[:, 0:d_model]
    scale = mod_ref[:, d_model:2 * d_model]
    h = (y * (1.0 + scale) + shift).astype(BF16)
    lane = lax.broadcasted_iota(jnp.int32, (x.shape[0], LANES), 1)
    low_half = (lane & 16) == 0
    for (width, rope, _, _), w_ref, o_ref in zip(groups, w_refs, out_refs):
        r = jnp.dot(h, w_ref[...], preferred_element_type=F32)
        if rope is None:
            o_ref[...] = r.astype(o_ref.dtype)
        else:
            cos_ref, sin_ref = (qcos_ref, qsin_ref) if rope == "q" else (kcos_ref, ksin_ref)
            cos = cos_ref[...]
            sin = sin_ref[...]
            for s in range(width // LANES):
                sl = slice(s * LANES, (s + 1) * LANES)
                o_ref[:, sl] = _rope_slab(r[:, sl], cos, sin, low_half).astype(o_ref.dtype)


def _project(x2, mod3, g_pre, w, tables, groups, rows_per_batch, mod_row_of_tile):
    n, d = x2.shape
    tm = min(PROJ_TM, rows_per_batch)
    tiles_per_batch = rows_per_batch // tm
    n_batches = n // rows_per_batch
    tab = pl.BlockSpec((tm, LANES), lambda i: (i % tiles_per_batch, 0))
    out_cols = sum(g[0] for g in groups)

    def out_spec(width, time_major):
        if time_major:
            return pl.BlockSpec((tm, width), lambda i: (i % tiles_per_batch, i // tiles_per_batch))
        return pl.BlockSpec((tm, width), lambda i: (i, 0))

    def out_shape(width, time_major):
        shape = (rows_per_batch, n_batches * width) if time_major else (n, width)
        return jax.ShapeDtypeStruct(shape, BF16)

    est = (d * out_cols * 2 + 2 * tm * d * 4 + 2 * tm * out_cols * 2 + 6 * tm * d * 4
           + 8 * tm * LANES * 4 + (4 << 20))

    def w_cols(width, offset):
        assert offset % width == 0
        return pl.BlockSpec((d, width), lambda i: (0, offset // width), pipeline_mode=pl.Buffered(1))

    return pl.pallas_call(
        functools.partial(_proj_kernel, d_model=d, groups=groups),
        grid=(n // tm,),
        in_specs=[
            pl.BlockSpec((tm, d), lambda i: (i, 0)),
            pl.BlockSpec((None, 1, 3 * d), lambda i: (mod_row_of_tile(i, tiles_per_batch), 0, 0)),
            pl.BlockSpec((1, d), lambda i: (0, 0)),
            tab, tab, tab, tab,
        ] + [w_cols(wd, off) for wd, _, off, _ in groups],
        out_specs=[out_spec(wd, tmaj) for wd, _, _, tmaj in groups],
        out_shape=[out_shape(wd, tmaj) for wd, _, _, tmaj in groups],
        compiler_params=pltpu.CompilerParams(
            dimension_semantics=("arbitrary",), vmem_limit_bytes=_vmem_limit(est)),
        name="proj",
    )(x2, mod3, g_pre, *tables, *([w] * len(groups)))


def _attn_kernel(q_ref, kc_ref, k_ref, vc_ref, v_ref, lam_ref, g_ref, o_ref, vext_ref,
                 s0_ref, s1_ref, *, n_ctx, rows):
    n_sub = q_ref.shape[0] // rows
    n_heads = q_ref.shape[1] // V_DIM
    cols = [slice(h * V_DIM, (h + 1) * V_DIM) for h in range(n_heads)]
    for h in range(n_heads):
        vext_ref[h, 0:n_ctx, 0:V_DIM] = vc_ref[:, cols[h]]
        vext_ref[h, n_ctx:, 0:V_DIM] = v_ref[:, cols[h]]
        vext_ref[h, :, V_DIM:] = jnp.ones((vext_ref.shape[1], V_DIM), BF16)
    lane = lax.broadcasted_iota(jnp.int32, (rows, V_DIM), 1)
    first_half = lane < HEAD_DIM
    blocks = [(h, r) for h in range(n_heads) for r in range(n_sub)]
    s_refs = (s0_ref, s1_ref)

    def scores(idx):
        h, r = blocks[idx]
        q = q_ref[r * rows:(r + 1) * rows, cols[h]]
        zero = jnp.zeros_like(q)
        q2 = jnp.concatenate([jnp.where(first_half, q, zero), jnp.where(first_half, zero, q)], axis=0)
        nt = (((1,), (1,)), ((), ()))
        s_refs[idx % 2][:, 0:n_ctx] = lax.dot_general(q2, kc_ref[:, cols[h]], nt, preferred_element_type=F32)
        s_refs[idx % 2][:, n_ctx:] = lax.dot_general(q2, k_ref[:, cols[h]], nt, preferred_element_type=F32)

    def attend(idx):
        h, r = blocks[idx]
        s = s_refs[idx % 2][...]
        m = jnp.max(s, axis=-1, keepdims=True)
        e = jnp.exp2(s - m).astype(BF16)
        acc = jnp.dot(e, vext_ref[h], preferred_element_type=F32)
        oj = acc[:, 0:V_DIM] / acc[:, V_DIM:]
        o = oj[0:rows] - lam_ref[...] * oj[rows:]
        ms = jnp.mean(o * o, axis=-1, keepdims=True)
        o = o * lax.rsqrt(ms + NORM_EPS) * g_ref[...] * (1.0 - LAMBDA_INIT)
        o_ref[r * rows:(r + 1) * rows, cols[h]] = o.astype(o_ref.dtype)

    scores(0)
    for idx in range(len(blocks)):
        if idx + 1 < len(blocks):
            scores(idx + 1)
        attend(idx)


def _attention(q, kc, k, vc, v, lam_row, g_subln, batch, n_lat, n_ctx):
    rows = ATTN_ROWS
    nh = ATTN_HEADS
    assert n_lat % rows == 0 and HEADS % nh == 0
    n_keys = n_ctx + n_lat
    est = (nh * (2 * (n_keys * V_DIM * 2 * 2) + n_keys * V_DIM * 2 + n_keys * 2 * V_DIM * 2
                 + 4 * n_lat * V_DIM * 2) + 2 * (2 * rows * n_keys * 4) + 4 * (2 * rows * n_keys * 4) + (8 << 20))

    def heads(n_rows):
        return pl.BlockSpec((n_rows, nh * V_DIM), lambda b, h: (b, h))

    return pl.pallas_call(
        functools.partial(_attn_kernel, n_ctx=n_ctx, rows=rows),
        grid=(batch, HEADS // nh),
        in_specs=[
            heads(n_lat), heads(n_ctx), heads(n_lat), heads(n_ctx), heads(n_lat),
            pl.BlockSpec((1, LANES), lambda b, h: (0, 0)),
            pl.BlockSpec((1, V_DIM), lambda b, h: (0, 0)),
        ],
        out_specs=heads(n_lat),
        out_shape=jax.ShapeDtypeStruct((batch * n_lat, HEADS * V_DIM), BF16),
        scratch_shapes=[
            pltpu.VMEM((nh, n_keys, 2 * V_DIM), BF16),
            pltpu.VMEM((2 * rows, n_keys), F32),
            pltpu.VMEM((2 * rows, n_keys), F32),
        ],
        compiler_params=pltpu.CompilerParams(
            dimension_semantics=("arbitrary", "arbitrary"),
            vmem_limit_bytes=_vmem_limit(est)),
        name="attn",
    )(q, kc, k, vc, v, lam_row, g_subln)


def _scan_kernel(xf_ref, xfp_ref, xfn_ref, xb_ref, xbp_ref, xbn_ref, cw_ref, cb_ref, wg_ref, bg_ref,
                 lam_ref, h0_ref, hf_ref, hb_ref, hfin_ref, xe_ref, wg_scr, a_ref, u_ref, st_ref,
                 *, batch, tc, wc):
    i = pl.program_id(1)
    n = pl.num_programs(1)
    t_grp = LRU_GATE_ROWS // batch
    n_blk, blk = wg_ref.shape[2], wg_ref.shape[3]

    @pl.when(i == 0)
    def _():
        st_ref[...] = h0_ref[...]
        wg_scr[...] = jnp.zeros(wg_scr.shape, BF16)
        for d in range(2):
            for gate in range(2):
                for m in range(n_blk):
                    wg_scr[d, m * blk:(m + 1) * blk, gate * wc + m * blk:gate * wc + (m + 1) * blk] = (
                        wg_ref[d, gate, m].astype(BF16))

    half_taps = [(0.5 * cw_ref[j:j + 1, :]).astype(BF16) for j in range(CONV_WIDTH)]
    half_cb = (0.5 * cb_ref[...]).astype(BF16)

    def coeffs(d, cur_ref, prev_ref, next_ref, chunk):
        prev = prev_ref[...]
        nxt = next_ref[...]
        xe_ref[d, 1:tc + 1] = cur_ref[...]
        xe_ref[d, 0:1] = jnp.where(chunk == 0, jnp.zeros_like(prev), prev)
        xe_ref[d, tc + 1:tc + 3] = jnp.where(chunk == n - 1, jnp.zeros_like(nxt), nxt)
        half_cl = (-0.5 * LRU_C * math.log2(math.e)) * jax.nn.softplus(-lam_ref[d])
        half_bg = 0.5 * bg_ref[d]
        for t0 in range(0, tc, t_grp):
            xh = half_cb
            for j in range(CONV_WIDTH):
                xh = xh + xe_ref[d, t0 + j:t0 + j + t_grp] * half_taps[j]
            xh = xh.reshape(t_grp * batch, wc)
            z = jnp.dot(xh, wg_scr[d], preferred_element_type=F32) + half_bg
            xh = xh.astype(F32)
            a = jnp.exp2(half_cl * jnp.tanh(z[:, 0:wc]) + half_cl)
            y = 1.0 - a * a
            mult = y * lax.rsqrt(jnp.maximum(y, 1e-30))
            u = (mult * xh) * (jnp.tanh(z[:, wc:]) + 1.0)
            a_ref[d, t0:t0 + t_grp] = a.reshape(t_grp, batch, wc)
            u_ref[d, t0:t0 + t_grp] = u.reshape(t_grp, batch, wc)

    coeffs(0, xf_ref, xfp_ref, xfn_ref, i)
    coeffs(1, xb_ref, xbp_ref, xbn_ref, n - 1 - i)

    def scan_step(t, hs):
        tb = tc - 1 - t
        h_f = a_ref[0, t] * hs[0] + u_ref[0, t]
        hf_ref[t] = h_f.astype(hf_ref.dtype)
        h_b = a_ref[1, tb] * hs[1] + u_ref[1, tb]
        hb_ref[tb] = h_b.astype(hb_ref.dtype)
        return h_f, h_b

    h_f, h_b = lax.fori_loop(0, tc, scan_step, (st_ref[0], st_ref[1]), unroll=8)
    st_ref[0] = h_f
    st_ref[1] = h_b

    @pl.when(i == n - 1)
    def _():
        hfin_ref[...] = st_ref[...]


def _scan(x_t, conv_w, conv_b, wg, bg, lam, h0):
    length, batch, width = x_t.shape
    tc, wc = LRU_TC, LRU_WC
    n = length // tc
    blk = wg.shape[-1]
    assert tc % 2 == 0 and LRU_GATE_ROWS % batch == 0 and tc % (LRU_GATE_ROWS // batch) == 0
    assert wc % blk == 0 and wg.shape[2] * blk == width

    def chunk(fwd, i):
        return i if fwd else n - 1 - i

    def cur(fwd):
        return pl.BlockSpec((tc, batch, wc), lambda s, i: (chunk(fwd, i), 0, s))

    def prev(fwd):
        return pl.BlockSpec((1, batch, wc), lambda s, i: (jnp.maximum(chunk(fwd, i) * tc - 1, 0), 0, s))

    def nxt(fwd):
        return pl.BlockSpec((2, batch, wc), lambda s, i: (
            jnp.minimum((chunk(fwd, i) + 1) * (tc // 2), length // 2 - 1), 0, s))

    chunk_f32 = 2 * tc * batch * wc * 4
    est = (2 * (tc + 3) * batch * wc * 4 + 2 * chunk_f32 + 8 * tc * batch * wc * 2
           + 8 * LRU_GATE_ROWS * 2 * wc * 4 + (8 << 20))
    return pl.pallas_call(
        functools.partial(_scan_kernel, batch=batch, tc=tc, wc=wc),
        grid=(width // wc, n),
        in_specs=[
            cur(True), prev(True), nxt(True), cur(False), prev(False), nxt(False),
            pl.BlockSpec((CONV_WIDTH, wc), lambda s, i: (0, s)),
            pl.BlockSpec((1, wc), lambda s, i: (0, s)),
            pl.BlockSpec((2, 2, wc // blk, blk, blk), lambda s, i: (0, 0, s, 0, 0)),
            pl.BlockSpec((2, None, 1, 2 * wc), lambda s, i: (0, s, 0, 0)),
            pl.BlockSpec((2, 1, wc), lambda s, i: (0, 0, s)),
            pl.BlockSpec((2, batch, wc), lambda s, i: (0, 0, s)),
        ],
        out_specs=[cur(True), cur(False), pl.BlockSpec((2, batch, wc), lambda s, i: (0, 0, s))],
        out_shape=[
            jax.ShapeDtypeStruct((length, batch, width), BF16),
            jax.ShapeDtypeStruct((length, batch, width), BF16),
            jax.ShapeDtypeStruct((2, batch, width), F32),
        ],
        scratch_shapes=[
            pltpu.VMEM((2, tc + 3, batch, wc), BF16),
            pltpu.VMEM((2, wc, 2 * wc), BF16),
            pltpu.VMEM((2, tc, batch, wc), F32),
            pltpu.VMEM((2, tc, batch, wc), F32),
            pltpu.VMEM((2, batch, wc), F32),
        ],
        compiler_params=pltpu.CompilerParams(
            dimension_semantics=("arbitrary", "arbitrary"), vmem_limit_bytes=_vmem_limit(est)),
        name="lru",
    )(x_t, x_t, x_t, x_t, x_t, x_t, conv_w, conv_b, wg, bg, lam, h0)


def _merge_kernel(attn_ref, ga_ref, hf_ref, hb_ref, gr_ref, gm_ref, x_ref, mod_ref, gp_ref,
                  wa_ref, wl_ref, wo_ref, o_ref, *, d_model):
    def silu(ref):
        h = ref[...] * 0.5
        return h + h * jnp.tanh(h)

    a_in = attn_ref[...] * silu(ga_ref)
    y_attn = jnp.dot(a_in, wa_ref[...], preferred_element_type=F32)
    l_in = (hf_ref[...] + hb_ref[...]) * silu(gr_ref)
    y_lru = jnp.dot(l_in, wl_ref[...], preferred_element_type=F32)
    t_attn = jnp.tanh(0.5 * gm_ref[:, 0:d_model].astype(F32))
    t_lru = jnp.tanh(0.5 * gm_ref[:, d_model:].astype(F32))
    z = (0.5 * ((t_attn + 1.0) * y_attn + (t_lru + 1.0) * y_lru)).astype(BF16)
    y = jnp.dot(z, wo_ref[...], preferred_element_type=F32)
    ms = jnp.mean(y * y, axis=-1, keepdims=True)
    yn = y * lax.rsqrt(ms + NORM_EPS) * gp_ref[...]
    gate = mod_ref[:, 2 * d_model:]
    o_ref[...] = x_ref[...] + gate * yn


def _merge(attn, ga, hf, hb, gr, gm, x2, mod3, g_post, wa, wl, wo, rows_per_batch):
    n, d = x2.shape
    tm = MERGE_TM
    tiles_per_batch = rows_per_batch // tm
    row = pl.BlockSpec((tm, d), lambda i: (i, 0))
    trow = pl.BlockSpec((tm, d), lambda i: (i % tiles_per_batch, i // tiles_per_batch))
    wspec = pl.BlockSpec((d, d), lambda i: (0, 0), pipeline_mode=pl.Buffered(1))
    est = 3 * d * d * 2 + 2 * (5 * tm * d * 2 + tm * 2 * d * 2 + 2 * tm * d * 4) + 10 * tm * d * 4 + (4 << 20)
    return pl.pallas_call(
        functools.partial(_merge_kernel, d_model=d),
        grid=(n // tm,),
        in_specs=[
            row, row, trow, trow, row,
            pl.BlockSpec((tm, 2 * d), lambda i: (i, 0)),
            row,
            pl.BlockSpec((None, 1, 3 * d), lambda i: (i // tiles_per_batch, 0, 0)),
            pl.BlockSpec((1, d), lambda i: (0, 0)),
            wspec, wspec, wspec,
        ],
        out_specs=row,
        out_shape=jax.ShapeDtypeStruct((n, d), F32),
        compiler_params=pltpu.CompilerParams(
            dimension_semantics=("arbitrary",), vmem_limit_bytes=_vmem_limit(est)),
        name="merge",
    )(attn, ga, hf, hb, gr, gm, x2, mod3, g_post, wa, wl, wo)


def _rope_tables(n_tokens):
    t = np.arange(n_tokens)
    row = (t // GRID_W).astype(np.float64)
    col = (t % GRID_W).astype(np.float64)
    axis_dim = HEAD_DIM // 2
    inv_freq = ROPE_THETA ** (-np.arange(0, axis_dim, 2, dtype=np.float64) / axis_dim)
    d = np.arange(LANES) % HEAD_DIM
    freq = inv_freq[d % (axis_dim // 2)]
    pos = np.where((d < axis_dim)[None, :], row[:, None], col[:, None])
    ang = pos * freq[None, :]
    sign = np.where((d % axis_dim) < axis_dim // 2, -1.0, 1.0)
    cos = np.cos(ang)
    sin = np.sin(ang) * sign[None, :]
    qs = (HEAD_DIM ** -0.5) * math.log2(math.e)
    return tuple(jnp.asarray(a, F32) for a in (cos * qs, sin * qs, cos, sin))


def _gate_params(w_a, w_x, b_a, b_x, wc):
    n_slab = b_a.shape[-1] // wc
    wg = jnp.stack([w_a, w_x], axis=1)
    bg = jnp.concatenate([b_a.reshape(2, n_slab, 1, wc), b_x.reshape(2, n_slab, 1, wc)], axis=-1)
    return wg, bg


def kernel(x, c, ctx, c_ctx, w_mod, b_mod, g_pre, g_post, w_in, lambda_q1, lambda_k1, lambda_q2, lambda_k2,
           g_subln, w_attn_out, conv_w, conv_b, w_rg_a, b_rg_a, w_rg_x, b_rg_x, lru_lambda, w_lru_out, w_out):
    batch, n_lat, d = x.shape
    n_ctx = ctx.shape[1]
    assert w_mod.shape[0] == 1, "single-layer block"
    assert d == HEADS * V_DIM and n_lat % PROJ_TM == 0 and n_lat % MERGE_TM == 0 and n_ctx % BF16_ROWS == 0

    c_all = jnp.zeros((MOD_ROWS, d), F32).at[:batch].set(c).at[batch].set(c_ctx)
    vec = lambda a: a[0].reshape(1, HEAD_DIM)
    mod, lam_row = _modulation(c_all, w_mod[0], b_mod[0].reshape(1, -1),
                               vec(lambda_q1), vec(lambda_k1), vec(lambda_q2), vec(lambda_k2))
    mod3 = mod.reshape(MOD_ROWS, 1, 3 * d)

    w_bf = w_in[0].astype(BF16)
    tables = _rope_tables(n_lat)
    g_pre2 = g_pre[0].reshape(1, d)
    x2 = x.reshape(batch * n_lat, d)
    lat_groups = ((d, "q", 0, False), (d, "k", d, False), (d, None, 2 * d, False), (d, None, 3 * d, False),
                  (d, None, 4 * d, True), (d, None, 5 * d, False), (2 * d, None, 6 * d, False))
    q, k, v, ga, xr, gr, gm = _project(
        x2, mod3, g_pre2, w_bf, tables, lat_groups, n_lat, lambda i, tpb: i // tpb)

    ctx_groups = ((d, None, d, False), (d, None, 2 * d, False), (d, None, 4 * d, True))
    kc, vc, xrc = _project(
        ctx.reshape(batch * n_ctx, d), mod3, g_pre2, w_bf, tables, ctx_groups, n_ctx,
        lambda i, tpb: batch)

    attn = _attention(q, kc, k, vc, v, lam_row, g_subln[0].reshape(1, V_DIM), batch, n_lat, n_ctx)

    wg, bg = _gate_params(w_rg_a[0], w_rg_x[0], b_rg_a[0], b_rg_x[0], LRU_WC)
    lam = lru_lambda[0].reshape(2, 1, d)
    cb = conv_b[0].reshape(1, d)
    zeros = jnp.zeros((2, batch, d), F32)
    _, _, h_ctx = _scan(xrc.reshape(n_ctx, batch, d), conv_w[0], cb, wg, bg, lam, zeros)
    hf, hb, _ = _scan(xr.reshape(n_lat, batch, d), conv_w[0], cb, wg, bg, lam, h_ctx)

    out = _merge(attn, ga, hf.reshape(n_lat, batch * d), hb.reshape(n_lat, batch * d), gr, gm, x2, mod3,
                 g_post[0].reshape(1, d), w_attn_out[0].astype(BF16), w_lru_out[0].astype(BF16),
                 w_out[0].astype(BF16), n_lat)
    return out.reshape(batch, n_lat, d)
```
